```python
import math
import jax, jax.numpy as jnp
from jax import lax
import numpy as np

D_MODEL = 1024
BATCH = 32
SEQ = 2048
DEPTH = 4
DEC_BATCH = 8
DEC_SEQ = 32
PAST_LEN = 1024

CHUNK = 64
Q_BLOCK = 128
D_MIX = D_MODEL
DA_HEADS = 4
DA_WIDTH = D_MIX // 2
DA_DV = DA_WIDTH // DA_HEADS
DA_DK = DA_DV // 2
ROT_DIM = DA_DK // 4
ROPE_THETA = 500000.0
GM_WIDTH = D_MIX - DA_WIDTH
GM_GROUPS = 4
GM_CH = GM_WIDTH // GM_GROUPS
GM_CHUNK = 128
MEM_LEN = 256
X_HEADS = 4
X_DH = D_MODEL // X_HEADS
D_FF = 2816
CONV_W = 3
EPS = 1e-6
DA_QK_COLS = DA_HEADS * 2 * DA_DK
IN_COLS = 2 * DA_QK_COLS + DA_WIDTH + 2 * GM_WIDTH
NEG = float(np.finfo(np.float32).min)

kernel_name = "hybrid_diffattn_gmlp_streaming_step"


def rmsnorm(x, g):
    xf = x.astype(jnp.float32)
    y = xf * lax.rsqrt(jnp.mean(xf * xf, axis=-1, keepdims=True) + EPS)
    return (y * g.astype(jnp.float32)).astype(x.dtype)


def partial_rope(x, pos):
    half = ROT_DIM // 2
    inv = ROPE_THETA ** (-jnp.arange(half, dtype=jnp.float32) / half)
    ang = pos.astype(jnp.float32)[:, None] * inv[None, :]
    cos = jnp.cos(ang)[:, None, None, :]
    sin = jnp.sin(ang)[:, None, None, :]
    x1 = x[..., :half].astype(jnp.float32)
    x2 = x[..., half:ROT_DIM].astype(jnp.float32)
    rot = jnp.concatenate([x1 * cos - x2 * sin, x2 * cos + x1 * sin], axis=-1)
    return jnp.concatenate([rot.astype(x.dtype), x[..., ROT_DIM:]], axis=-1)


def diff_attn_block(q, k, v, q_pos, k_pos, lam):
    s = jnp.einsum('bqhmd,bkhmd->bhmqk', q, k,
                   preferred_element_type=jnp.float32) * (DA_DK ** -0.5)
    mask = (k_pos[None, :] // CHUNK) <= (q_pos[:, None] // CHUNK)
    s = jnp.where(mask, s, NEG)
    p = jax.nn.softmax(s, axis=-1)
    a = p[:, :, 0] - lam * p[:, :, 1]
    return jnp.einsum('bhqk,bkhd->bqhd', a.astype(v.dtype), v)


def diff_attention(q, k, v, q_pos, k_pos, lam):
    B, Sq = q.shape[0], q.shape[1]
    if Sq > Q_BLOCK and Sq % Q_BLOCK == 0:
        nb = Sq // Q_BLOCK
        qb = q.reshape(B, nb, Q_BLOCK, DA_HEADS, 2, DA_DK).transpose(1, 0, 2, 3, 4, 5)
        pb = q_pos.reshape(nb, Q_BLOCK)
        out = lax.map(lambda a: diff_attn_block(a[0], k, v, a[1], k_pos, lam), (qb, pb))
        return out.transpose(1, 0, 2, 3, 4).reshape(B, Sq, DA_HEADS, DA_DV)
    return diff_attn_block(q, k, v, q_pos, k_pos, lam)


def spatial_gate(u, gv, w_s, b_s):
    B, S = u.shape[0], u.shape[1]
    L = min(S, GM_CHUNK)
    n = S // L
    tri = jnp.tril(jnp.ones((L, L), dtype=bool))
    w = jnp.where(tri[None], w_s[:, :L, :L], 0.0)
    vc = gv.reshape(B, n, L, GM_GROUPS, GM_CH)
    s = jnp.einsum('gts,bnsgc->bntgc', w, vc) + b_s[:, :L].T[None, None, :, :, None]
    return u * s.reshape(B, S, GM_WIDTH)


def mem_kv(mem, norm_g, wk, wv, kg):
    B, M, _ = mem.shape
    m = rmsnorm(mem, norm_g)
    k = rmsnorm((m @ wk).reshape(B, M, X_HEADS, X_DH), kg)
    v = (m @ wv).reshape(B, M, X_HEADS, X_DH)
    return k, v


def cross_attend(h, mk, mv, wq, wo, qg):
    B, S, _ = h.shape
    q = rmsnorm((h @ wq).reshape(B, S, X_HEADS, X_DH), qg)
    s = jnp.einsum('bqhd,bkhd->bhqk', q, mk,
                   preferred_element_type=jnp.float32) * (X_DH ** -0.5)
    p = jax.nn.softmax(s, axis=-1)
    o = jnp.einsum('bhqk,bkhd->bqhd', p.astype(mv.dtype), mv).reshape(B, S, D_MODEL)
    return o @ wo


def conv_ffn(h, hist, w_up, conv_w, conv_b, w_down):
    S = h.shape[1]
    g, up = jnp.split(h @ w_up, 2, axis=-1)
    gp = jnp.concatenate([hist, g], axis=1)
    c = conv_b + sum(conv_w[j] * gp[:, j:j + S] for j in range(CONV_W))
    a = jax.nn.silu(c) * up
    return a @ w_down, gp[:, -(CONV_W - 1):]


def trunk_layer(x, pos, past_k, past_v, mem_k, mem_v, conv_hist, w, lam_init):
    B, S, _ = x.shape
    h = rmsnorm(x, w['norm_mix_g'])
    z = h @ w['w_in']
    c1 = DA_QK_COLS
    c2 = 2 * DA_QK_COLS
    c3 = c2 + DA_WIDTH
    c4 = c3 + GM_WIDTH
    q, k, v, u, gv = jnp.split(z, [c1, c2, c3, c4], axis=-1)
    q = partial_rope(rmsnorm(q.reshape(B, S, DA_HEADS, 2, DA_DK), w['da_q_norm_g']), pos)
    k = partial_rope(rmsnorm(k.reshape(B, S, DA_HEADS, 2, DA_DK), w['da_k_norm_g']), pos)
    v = v.reshape(B, S, DA_HEADS, DA_DV)
    if past_k is None:
        k_all, v_all, k_pos = k, v, pos
    else:
        k_all = jnp.concatenate([past_k, k], axis=1)
        v_all = jnp.concatenate([past_v, v], axis=1)
        k_pos = jnp.arange(past_k.shape[1] + S)
    f32 = jnp.float32
    lam = (jnp.exp(jnp.sum(w['lambda_q1'].astype(f32) * w['lambda_k1'].astype(f32)))
           - jnp.exp(jnp.sum(w['lambda_q2'].astype(f32) * w['lambda_k2'].astype(f32)))
           + lam_init)
    o = diff_attention(q, k_all, v_all, pos, k_pos, lam)
    o = (rmsnorm(o, w['da_subln_g']) * (1.0 - lam_init)).reshape(B, S, DA_WIDTH)
    u = jax.nn.gelu(u)
    gv = rmsnorm(jax.nn.gelu(gv).reshape(B, S, GM_GROUPS, GM_CH), w['gm_norm_g'])
    g_out = spatial_gate(u, gv, w['gm_w_s'], w['gm_b'])
    x = x + jnp.concatenate([o, g_out], axis=-1) @ w['w_out']
    x = x + cross_attend(rmsnorm(x, w['norm_x_g']), mem_k, mem_v,
                         w['wq_c'], w['wo_c'], w['xq_norm_g'])
    f, conv_state = conv_ffn(rmsnorm(x, w['norm_ffn_g']), conv_hist,
                             w['w_up'], w['conv_w'], w['conv_b'], w['w_down'])
    return x + f, k, v, gv, conv_state


def setup_inputs(seed: int = 0) -> dict:
    key = jax.random.key(seed)
    ks = iter(jax.random.split(key, 40))
    f32 = jnp.float32

    def nrm(shape, scale):
        return scale * jax.random.normal(next(ks), shape, f32)

    def gain(shape):
        return 1.0 + 0.05 * jax.random.normal(next(ks), shape, f32)

    return {
        'x_prompt': nrm((BATCH, SEQ, D_MODEL), 1.0),
        'x_sample': nrm((DEC_BATCH, DEC_SEQ, D_MODEL), 1.0),
        'cache_da_k': nrm((DEPTH, DEC_BATCH, PAST_LEN, DA_HEADS, 2, DA_DK), 1.0),
        'cache_da_v': nrm((DEPTH, DEC_BATCH, PAST_LEN, DA_HEADS, DA_DV), 1.0),
        'cache_mem_k': nrm((DEPTH, DEC_BATCH, MEM_LEN, X_HEADS, X_DH), 1.0),
        'cache_mem_v': nrm((DEPTH, DEC_BATCH, MEM_LEN, X_HEADS, X_DH), 1.0),
        'state_ffn_conv': nrm((DEPTH, DEC_BATCH, CONV_W - 1, D_FF), 1.0),
        'mem_prompt': nrm((BATCH, MEM_LEN, D_MODEL), 1.0),
        'norm_mix_g': gain((DEPTH, D_MODEL)),
        'w_in': nrm((DEPTH, D_MODEL, IN_COLS), D_MODEL ** -0.5),
        'da_q_norm_g': gain((DEPTH, DA_DK)),
        'da_k_norm_g': gain((DEPTH, DA_DK)),
        'lambda_q1': nrm((DEPTH, DA_DK), 0.1),
        'lambda_k1': nrm((DEPTH, DA_DK), 0.1),
        'lambda_q2': nrm((DEPTH, DA_DK), 0.1),
        'lambda_k2': nrm((DEPTH, DA_DK), 0.1),
        'da_subln_g': gain((DEPTH, DA_DV)),
        'gm_norm_g': gain((DEPTH, GM_CH)),
        'gm_w_s': nrm((DEPTH, GM_GROUPS, GM_CHUNK, GM_CHUNK), GM_CHUNK ** -0.5),
        'gm_b': nrm((DEPTH, GM_GROUPS, GM_CHUNK), 0.02),
        'w_out': nrm((DEPTH, D_MIX, D_MODEL), D_MIX ** -0.5),
        'norm_x_g': gain((DEPTH, D_MODEL)),
        'norm_mem_g': gain((DEPTH, D_MODEL)),
        'wq_c': nrm((DEPTH, D_MODEL, D_MODEL), D_MODEL ** -0.5),
        'wk_c': nrm((DEPTH, D_MODEL, D_MODEL), D_MODEL ** -0.5),
        'wv_c': nrm((DEPTH, D_MODEL, D_MODEL), D_MODEL ** -0.5),
        'wo_c': nrm((DEPTH, D_MODEL, D_MODEL), D_MODEL ** -0.5),
        'xq_norm_g': gain((DEPTH, X_DH)),
        'xk_norm_g': gain((DEPTH, X_DH)),
        'norm_ffn_g': gain((DEPTH, D_MODEL)),
        'w_up': nrm((DEPTH, D_MODEL, 2 * D_FF), D_MODEL ** -0.5),
        'conv_w': nrm((DEPTH, CONV_W, D_FF), CONV_W ** -0.5),
        'conv_b': nrm((DEPTH, D_FF), 0.02),
        'w_down': nrm((DEPTH, D_FF, D_MODEL), D_FF ** -0.5),
    }


def reference(x_prompt, x_sample, cache_da_k, cache_da_v, cache_mem_k, cache_mem_v,
              state_ffn_conv, mem_prompt, norm_mix_g, w_in, da_q_norm_g, da_k_norm_g,
              lambda_q1, lambda_k1, lambda_q2, lambda_k2, da_subln_g, gm_norm_g,
              gm_w_s, gm_b, w_out, norm_x_g, norm_mem_g, wq_c, wk_c, wv_c, wo_c,
              xq_norm_g, xk_norm_g, norm_ffn_g, w_up, conv_w, conv_b, w_down):
    S_p = x_prompt.shape[1]
    S_s = x_sample.shape[1]
    past = cache_da_k.shape[2]
    pos_p = jnp.arange(S_p)
    pos_s = past + jnp.arange(S_s)
    hist_p = jnp.zeros((x_prompt.shape[0], CONV_W - 1, D_FF), x_prompt.dtype)

    xp, xs = x_prompt, x_sample
    dk_p, dv_p, mk_p, mv_p, fc_p = [], [], [], [], []
    dk_s, dv_s, gv_s, fc_s = [], [], [], []
    for l in range(DEPTH):
        lam_init = 0.8 - 0.6 * math.exp(-0.3 * l)
        w = dict(norm_mix_g=norm_mix_g[l], w_in=w_in[l], da_q_norm_g=da_q_norm_g[l],
                 da_k_norm_g=da_k_norm_g[l], lambda_q1=lambda_q1[l], lambda_k1=lambda_k1[l],
                 lambda_q2=lambda_q2[l], lambda_k2=lambda_k2[l], da_subln_g=da_subln_g[l],
                 gm_norm_g=gm_norm_g[l], gm_w_s=gm_w_s[l], gm_b=gm_b[l], w_out=w_out[l],
                 norm_x_g=norm_x_g[l], wq_c=wq_c[l], wo_c=wo_c[l], xq_norm_g=xq_norm_g[l],
                 norm_ffn_g=norm_ffn_g[l], w_up=w_up[l], conv_w=conv_w[l],
                 conv_b=conv_b[l], w_down=w_down[l])
        mk, mv = mem_kv(mem_prompt, norm_mem_g[l], wk_c[l], wv_c[l], xk_norm_g[l])
        xp, k_new, v_new, _, conv_new = trunk_layer(
            xp, pos_p, None, None, mk, mv, hist_p, w, lam_init)
        dk_p.append(k_new); dv_p.append(v_new); mk_p.append(mk); mv_p.append(mv)
        fc_p.append(conv_new)
        xs, k_new, v_new, gv_new, conv_new = trunk_layer(
            xs, pos_s, cache_da_k[l], cache_da_v[l], cache_mem_k[l], cache_mem_v[l],
            state_ffn_conv[l], w, lam_init)
        dk_s.append(k_new); dv_s.append(v_new); gv_s.append(gv_new); fc_s.append(conv_new)

    return (xp, xs, jnp.stack(dk_p), jnp.stack(dv_p), jnp.stack(mk_p), jnp.stack(mv_p),
            jnp.stack(fc_p), jnp.stack(dk_s), jnp.stack(dv_s), jnp.stack(gv_s),
            jnp.stack(fc_s))
```

```python
import functools
import math

import numpy as np
import jax
import jax.numpy as jnp
from jax import lax
from jax.experimental import pallas as pl
from jax.experimental.pallas import tpu as pltpu

F32 = jnp.float32
BF16 = jnp.bfloat16

D_MODEL = 1024
CHUNK = 64
DA_HEADS = 4
DA_DK = 64
DA_DV = 128
DA_QK_COLS = DA_HEADS * 2 * DA_DK
DA_WIDTH = DA_HEADS * DA_DV
ROT_DIM = 16
ROPE_THETA = 500000.0
GM_GROUPS = 4
GM_CH = 128
GM_WIDTH = GM_GROUPS * GM_CH
GM_CHUNK = 128
MEM_LEN = 256
X_HEADS = 4
X_DH = 256
D_FF = 2816
CONV_W = 3
EPS = 1e-6
IN_COLS = 2 * DA_QK_COLS + DA_WIDTH + 2 * GM_WIDTH
NEG = float(np.finfo(np.float32).min)

LANES = 128
VMEM_LIMIT_BYTES = 56 * 1024 * 1024
FF_CHUNK = 256
TOKEN_TILE = 512
Q_TILE = 256

_NT = (((1,), (1,)), ((), ()))


def _params(sem):
    return pltpu.CompilerParams(dimension_semantics=sem, vmem_limit_bytes=VMEM_LIMIT_BYTES)


def _const_spec(shape):
    n = len(shape)
    return pl.BlockSpec(shape, lambda *_: (0,) * n)


def _rms(x, g):
    return (x * lax.rsqrt(jnp.mean(x * x, axis=-1, keepdims=True) + EPS)) * g


def _gelu(x):
    return x * (0.5 * (1.0 + jnp.tanh(0.7978845608028654 * (x + 0.044715 * (x * x * x)))))


def _mix_in_body(x_ref, g_ref, w_ref, qkg_ref, gmg_ref, cos_ref, sa_ref, sb_ref,
                 q_ref, k_ref, v_ref, u_ref, gv_ref):
    h = _rms(x_ref[...], g_ref[...]).astype(BF16)
    z = jnp.dot(h, w_ref[...], preferred_element_type=F32)
    lane = lax.broadcasted_iota(jnp.int32, (1, LANES), 1)
    lo = lane < DA_DK
    cosv, sa, sb = cos_ref[...], sa_ref[...], sb_ref[...]
    for s in range(2 * DA_HEADS):
        cols = slice(s * LANES, (s + 1) * LANES)
        t = z[:, cols]
        sq = t * t
        s_lo = jnp.sum(jnp.where(lo, sq, 0.0), axis=-1, keepdims=True)
        s_hi = jnp.sum(jnp.where(lo, 0.0, sq), axis=-1, keepdims=True)
        ms = jnp.where(lo, s_lo, s_hi) * (1.0 / DA_DK)
        y = (t * lax.rsqrt(ms + EPS)) * qkg_ref[:, cols]
        y = y * cosv + pltpu.roll(y, LANES - ROT_DIM // 2, 1) * sa + pltpu.roll(y, ROT_DIM // 2, 1) * sb
        if s < DA_HEADS:
            q_ref[:, cols] = y.astype(q_ref.dtype)
        else:
            kc = slice((s - DA_HEADS) * LANES, (s - DA_HEADS + 1) * LANES)
            k_ref[:, kc] = y
    c2 = 2 * DA_QK_COLS
    v_ref[...] = z[:, c2:c2 + DA_WIDTH]
    c3 = c2 + DA_WIDTH
    u_ref[...] = _gelu(z[:, c3:c3 + GM_WIDTH]).astype(u_ref.dtype)
    c4 = c3 + GM_WIDTH
    for s in range(GM_GROUPS):
        cols = slice(s * GM_CH, (s + 1) * GM_CH)
        t = _gelu(z[:, c4 + s * GM_CH:c4 + (s + 1) * GM_CH])
        gv_ref[:, cols] = _rms(t, gmg_ref[:, cols]).astype(gv_ref.dtype)


def _mix_in(x, g, w, qkg, gmg, rope, tm, gv_dtype, name):
    t = x.shape[0]
    cos_t, sa_t, sb_t = rope
    nrope = cos_t.shape[0] // tm
    row = lambda i: (i, 0)
    rope_spec = pl.BlockSpec((tm, LANES), lambda i: (i % nrope, 0))
    return pl.pallas_call(
        _mix_in_body,
        grid=(t // tm,),
        in_specs=[pl.BlockSpec((tm, D_MODEL), row), _const_spec((1, D_MODEL)),
                  _const_spec((D_MODEL, IN_COLS)), _const_spec((1, 2 * DA_QK_COLS)),
                  _const_spec((1, GM_WIDTH)), rope_spec, rope_spec, rope_spec],
        out_specs=[pl.BlockSpec((tm, DA_QK_COLS), row), pl.BlockSpec((tm, DA_QK_COLS), row),
                   pl.BlockSpec((tm, DA_WIDTH), row), pl.BlockSpec((tm, GM_WIDTH), row),
                   pl.BlockSpec((tm, GM_WIDTH), row)],
        out_shape=[jax.ShapeDtypeStruct((t, DA_QK_COLS), BF16),
                   jax.ShapeDtypeStruct((t, DA_QK_COLS), F32),
                   jax.ShapeDtypeStruct((t, DA_WIDTH), F32),
                   jax.ShapeDtypeStruct((t, GM_WIDTH), BF16),
                   jax.ShapeDtypeStruct((t, GM_WIDTH), gv_dtype)],
        compiler_params=_params(("parallel",)),
        name=name,
    )(x, g, w, qkg, gmg, cos_t, sa_t, sb_t)


def _lambda(lq1, lk1, lq2, lk2, lam_init):
    a = jnp.sum(lq1[...] * lk1[...], axis=-1, keepdims=True)
    b = jnp.sum(lq2[...] * lk2[...], axis=-1, keepdims=True)
    return jnp.exp(a) - jnp.exp(b) + lam_init


def _subln(o, g, lam_init):
    return _rms(o, g) * (1.0 - lam_init)


def _da_prompt_body(lq1, lk1, lq2, lk2, subg_ref, q_ref, k_ref, v_ref, o_ref, *, lam_init, bq):
    seq = q_ref.shape[1]
    lam = _lambda(lq1, lk1, lq2, lk2, lam_init)
    qa = q_ref[0] * (DA_DK ** -0.5)
    ka = k_ref[0].astype(BF16)
    va = v_ref[0].astype(BF16)
    qm = (qa[:, :DA_DK], qa[:, DA_DK:])
    km = (ka[:, :DA_DK], ka[:, DA_DK:])
    mask = ((lax.broadcasted_iota(jnp.int32, (bq, bq), 1) // CHUNK)
            <= (lax.broadcasted_iota(jnp.int32, (bq, bq), 0) // CHUNK))
    for qi in range(seq // bq):
        lo, hi = qi * bq, (qi + 1) * bq
        probs = []
        for m in range(2):
            q = qm[m][lo:hi]
            sd = jnp.where(mask, lax.dot_general(q, km[m][lo:hi], _NT, preferred_element_type=F32), NEG)
            mx = jnp.max(sd, axis=-1, keepdims=True)
            if qi > 0:
                so = lax.dot_general(q, km[m][:lo], _NT, preferred_element_type=F32)
                mx = jnp.maximum(mx, jnp.max(so, axis=-1, keepdims=True))
            pd = jnp.exp(sd - mx)
            den = jnp.sum(pd, axis=-1, keepdims=True)
            po = None
            if qi > 0:
                po = jnp.exp(so - mx)
                den = den + jnp.sum(po, axis=-1, keepdims=True)
            probs.append((pd, po, den))
        c1 = 1.0 / probs[0][2]
        c2 = lam / probs[1][2]
        ad = (probs[0][0] * c1 - probs[1][0] * c2).astype(BF16)
        o = jnp.dot(ad, va[lo:hi], preferred_element_type=F32)
        if qi > 0:
            ao = (probs[0][1] * c1 - probs[1][1] * c2).astype(BF16)
            o = o + jnp.dot(ao, va[:lo], preferred_element_type=F32)
        o_ref[0, lo:hi, :] = _subln(o, subg_ref[...], lam_init).astype(o_ref.dtype)


def _da_prompt(q, k, v, lams, subg, lam_init, name):
    b, s, _ = q.shape
    lam_spec = _const_spec((1, DA_DK))
    head_spec = pl.BlockSpec((1, s, LANES), lambda bi, h: (bi, 0, h))
    return pl.pallas_call(
        functools.partial(_da_prompt_body, lam_init=lam_init, bq=Q_TILE),
        grid=(b, DA_HEADS),
        in_specs=[lam_spec, lam_spec, lam_spec, lam_spec, _const_spec((1, DA_DV)),
                  head_spec, head_spec, head_spec],
        out_specs=head_spec,
        out_shape=jax.ShapeDtypeStruct((b, s, DA_WIDTH), BF16),
        compiler_params=_params(("parallel", "parallel")),
        name=name,
    )(*lams, subg, q, k, v)


def _da_sample_body(lq1, lk1, lq2, lk2, subg_ref, q_ref, kp_ref, kn_ref, vp_ref, vn_ref, o_ref,
                    *, lam_init, past, sq):
    lam = _lambda(lq1, lk1, lq2, lk2, lam_init)
    q = q_ref[0] * (DA_DK ** -0.5)
    kp = kp_ref[0, 0].astype(BF16)
    kn = kn_ref[0].astype(BF16)
    q_chunk = (past + lax.broadcasted_iota(jnp.int32, (sq, 1), 0)) // CHUNK
    mask_p = (lax.broadcasted_iota(jnp.int32, (1, past), 1) // CHUNK) <= q_chunk
    mask_n = ((past + lax.broadcasted_iota(jnp.int32, (1, sq), 1)) // CHUNK) <= q_chunk
    probs = []
    for m in range(2):
        d = slice(m * DA_DK, (m + 1) * DA_DK)
        sp = lax.dot_general(q[:, d], kp[:, d], _NT, preferred_element_type=F32)
        sn = lax.dot_general(q[:, d], kn[:, d], _NT, preferred_element_type=F32)
        sp = jnp.where(mask_p, sp, NEG)
        sn = jnp.where(mask_n, sn, NEG)
        mx = jnp.maximum(jnp.max(sp, axis=-1, keepdims=True), jnp.max(sn, axis=-1, keepdims=True))
        pp = jnp.exp(sp - mx)
        pn = jnp.exp(sn - mx)
        inv = 1.0 / (jnp.sum(pp, axis=-1, keepdims=True) + jnp.sum(pn, axis=-1, keepdims=True))
        probs.append((pp * inv, pn * inv))
    ap = (probs[0][0] - lam * probs[1][0]).astype(BF16)
    an = (probs[0][1] - lam * probs[1][1]).astype(BF16)
    o = (jnp.dot(ap, vp_ref[0, 0].astype(BF16), preferred_element_type=F32)
         + jnp.dot(an, vn_ref[0].astype(BF16), preferred_element_type=F32))
    o_ref[0] = _subln(o, subg_ref[...], lam_init).astype(o_ref.dtype)


def _da_sample(q, k_new, v_new, cache_k, cache_v, layer, lams, subg, lam_init, name):
    b, sq, _ = q.shape
    past = cache_k.shape[2]
    lam_spec = _const_spec((1, DA_DK))
    new_spec = pl.BlockSpec((1, sq, LANES), lambda bi, h: (bi, 0, h))
    past_spec = pl.BlockSpec((1, 1, past, LANES), lambda bi, h: (layer, bi, 0, h))
    return pl.pallas_call(
        functools.partial(_da_sample_body, lam_init=lam_init, past=past, sq=sq),
        grid=(b, DA_HEADS),
        in_specs=[lam_spec, lam_spec, lam_spec, lam_spec, _const_spec((1, DA_DV)),
                  new_spec, past_spec, new_spec, past_spec, new_spec],
        out_specs=new_spec,
        out_shape=jax.ShapeDtypeStruct((b, sq, DA_WIDTH), BF16),
        compiler_params=_params(("parallel", "parallel")),
        name=name,
    )(*lams, subg, q, cache_k, k_new, cache_v, v_new)


def _mix_out_body(x_ref, o_ref, u_ref, gv_ref, ws_ref, bs_ref, wout_ref, xo_ref, cat_ref, *, chunk):
    tm = x_ref.shape[0]
    tri = (lax.broadcasted_iota(jnp.int32, (chunk, chunk), 1)
           <= lax.broadcasted_iota(jnp.int32, (chunk, chunk), 0))
    cat_ref[:, :DA_WIDTH] = o_ref[...]
    for g in range(GM_GROUPS):
        cols = slice(g * GM_CH, (g + 1) * GM_CH)
        w = jnp.where(tri, ws_ref[g], 0.0).astype(BF16)
        bias = bs_ref[g]
        for c in range(tm // chunk):
            rows = slice(c * chunk, (c + 1) * chunk)
            s = jnp.dot(w, gv_ref[rows, cols].astype(BF16), preferred_element_type=F32) + bias
            cat_ref[rows, DA_WIDTH + g * GM_CH:DA_WIDTH + (g + 1) * GM_CH] = (
                u_ref[rows, cols].astype(F32) * s).astype(BF16)
    xo_ref[...] = x_ref[...] + jnp.dot(cat_ref[...], wout_ref[...], preferred_element_type=F32)


def _mix_out(x, o, u, gv, ws, bs, wout, tm, chunk, name):
    t = x.shape[0]
    row = lambda i: (i, 0)
    half = pl.BlockSpec((tm, DA_WIDTH), row)
    return pl.pallas_call(
        functools.partial(_mix_out_body, chunk=chunk),
        grid=(t // tm,),
        in_specs=[pl.BlockSpec((tm, D_MODEL), row), half, half, half,
                  _const_spec((GM_GROUPS, chunk, chunk)), _const_spec((GM_GROUPS, chunk, GM_CH)),
                  _const_spec((D_MODEL, D_MODEL))],
        out_specs=pl.BlockSpec((tm, D_MODEL), row),
        out_shape=jax.ShapeDtypeStruct((t, D_MODEL), F32),
        scratch_shapes=[pltpu.VMEM((tm, D_MODEL), BF16)],
        compiler_params=_params(("parallel",)),
        name=name,
    )(x, o, u, gv, ws, bs, wout)


def _mem_kv_body(mem_ref, g_ref, wk_ref, wv_ref, kg_ref, k_ref, v_ref):
    m = _rms(mem_ref[...], g_ref[...]).astype(BF16)
    k = jnp.dot(m, wk_ref[...], preferred_element_type=F32)
    for h in range(X_HEADS):
        cols = slice(h * X_DH, (h + 1) * X_DH)
        k_ref[:, cols] = _rms(k[:, cols], kg_ref[...])
    v_ref[...] = jnp.dot(m, wv_ref[...], preferred_element_type=F32)


def _mem_kv(mem, g, wk, wv, kg, tm, name):
    t = mem.shape[0]
    row = lambda i: (i, 0)
    blk = pl.BlockSpec((tm, D_MODEL), row)
    return pl.pallas_call(
        _mem_kv_body,
        grid=(t // tm,),
        in_specs=[blk, _const_spec((1, D_MODEL)), _const_spec((D_MODEL, D_MODEL)),
                  _const_spec((D_MODEL, D_MODEL)), _const_spec((1, X_DH))],
        out_specs=[blk, blk],
        out_shape=[jax.ShapeDtypeStruct((t, D_MODEL), F32)] * 2,
        compiler_params=_params(("parallel",)),
        name=name,
    )(mem, g, wk, wv, kg)


def _xattn_body(x_ref, g_ref, wq_ref, qg_ref, mk_ref, mv_ref, wo_ref, xo_ref, att_ref, *, nb, rows_per_b):
    x = x_ref[...]
    h = _rms(x, g_ref[...]).astype(BF16)
    q = jnp.dot(h, wq_ref[...], preferred_element_type=F32)
    for hh in range(X_HEADS):
        cols = slice(hh * X_DH, (hh + 1) * X_DH)
        qn = (_rms(q[:, cols], qg_ref[...]) * (X_DH ** -0.5)).astype(BF16)
        for bi in range(nb):
            rows = slice(bi * rows_per_b, (bi + 1) * rows_per_b)
            if len(mk_ref.shape) == 4:
                mk, mv = mk_ref[0, bi, :, cols], mv_ref[0, bi, :, cols]
            else:
                mk, mv = mk_ref[bi, :, cols], mv_ref[bi, :, cols]
            s = lax.dot_general(qn[rows], mk.astype(BF16), _NT, preferred_element_type=F32)
            p = jnp.exp(s - jnp.max(s, axis=-1, keepdims=True))
            p = p * (1.0 / jnp.sum(p, axis=-1, keepdims=True))
            att_ref[rows, cols] = jnp.dot(p.astype(BF16), mv.astype(BF16),
                                          preferred_element_type=F32).astype(BF16)
    xo_ref[...] = x + jnp.dot(att_ref[...], wo_ref[...], preferred_element_type=F32)


def _xattn(x, g, wq, qg, mk, mv, mem_spec, wo, tm, nb, name):
    t = x.shape[0]
    row = lambda i: (i, 0)
    blk = pl.BlockSpec((tm, D_MODEL), row)
    return pl.pallas_call(
        functools.partial(_xattn_body, nb=nb, rows_per_b=tm // nb),
        grid=(t // tm,),
        in_specs=[blk, _const_spec((1, D_MODEL)), _const_spec((D_MODEL, D_MODEL)),
                  _const_spec((1, X_DH)), mem_spec, mem_spec, _const_spec((D_MODEL, D_MODEL))],
        out_specs=blk,
        out_shape=jax.ShapeDtypeStruct((t, D_MODEL), F32),
        scratch_shapes=[pltpu.VMEM((tm, D_MODEL), BF16)],
        compiler_params=_params(("parallel",)),
        name=name,
    )(x, g, wq, qg, mk, mv, wo)


def _conv_ffn_body(x_ref, g_ref, wup_ref, cw_ref, cb_ref, wdn_ref, hist_ref, xo_ref, cst_ref,
                   gp_ref, carry_ref, act_ref, *, nb, rows_per_b, tiles_per_b):
    i = pl.program_id(0)
    r = rows_per_b
    x = x_ref[...]
    h = _rms(x, g_ref[...]).astype(BF16)
    first = (i % tiles_per_b) == 0
    for c in range(D_FF // FF_CHUNK):
        cs = slice(c * FF_CHUNK, (c + 1) * FF_CHUNK)
        g = jnp.dot(h, wup_ref[:, cs], preferred_element_type=F32)
        up = jnp.dot(h, wup_ref[:, D_FF + c * FF_CHUNK:D_FF + (c + 1) * FF_CHUNK],
                     preferred_element_type=F32)
        for bi in range(nb):
            rows = slice(bi * r, (bi + 1) * r)
            if tiles_per_b == 1:
                prev = hist_ref[bi, :, cs]
            else:
                prev = jnp.where(first, hist_ref[bi, :, cs], carry_ref[:, cs])
            gp_ref[6:8, :] = prev
            gp_ref[8:8 + r, :] = g[rows]
            conv = (cb_ref[:, cs] + cw_ref[0:1, cs] * gp_ref[6:6 + r, :]
                    + cw_ref[1:2, cs] * gp_ref[7:7 + r, :] + cw_ref[2:3, cs] * g[rows])
            act = conv * (1.0 / (1.0 + jnp.exp(-conv))) * up[rows]
            act_ref[rows, cs] = act.astype(BF16)
            last2 = gp_ref[6 + r:8 + r, :]
            cst_ref[bi, :, cs] = last2
            if tiles_per_b > 1:
                carry_ref[:, cs] = last2
    xo_ref[...] = x + jnp.dot(act_ref[...], wdn_ref[...], preferred_element_type=F32)


def _conv_ffn(x, g, wup, cw, cb, wdn, hist, tm, nb, tiles_per_b, name):
    t = x.shape[0]
    row = lambda i: (i, 0)
    blk = pl.BlockSpec((tm, D_MODEL), row)
    hist_spec = pl.BlockSpec((nb, CONV_W - 1, D_FF), lambda i: (i // tiles_per_b, 0, 0))
    return pl.pallas_call(
        functools.partial(_conv_ffn_body, nb=nb, rows_per_b=tm // nb, tiles_per_b=tiles_per_b),
        grid=(t // tm,),
        in_specs=[blk, _const_spec((1, D_MODEL)), _const_spec((D_MODEL, 2 * D_FF)),
                  _const_spec((CONV_W, D_FF)), _const_spec((1, D_FF)), _const_spec((D_FF, D_MODEL)),
                  hist_spec],
        out_specs=[blk, hist_spec],
        out_shape=[jax.ShapeDtypeStruct((t, D_MODEL), F32),
                   jax.ShapeDtypeStruct(hist.shape, F32)],
        scratch_shapes=[pltpu.VMEM((tm // nb + 8, FF_CHUNK), F32),
                        pltpu.VMEM((CONV_W - 1, D_FF), F32),
                        pltpu.VMEM((tm, D_FF), BF16)],
        compiler_params=_params(("arbitrary",)),
        name=name,
    )(x, g, wup, cw, cb, wdn, hist)


def _rope_tables(pos):
    half = ROT_DIM // 2
    inv = ROPE_THETA ** (-jnp.arange(half, dtype=F32) / half)
    ang = pos.astype(F32)[:, None] * inv[None, :]
    cos, sin = jnp.cos(ang), jnp.sin(ang)
    n = pos.shape[0]
    ones = jnp.ones((n, DA_DK - ROT_DIM), F32)
    zeros_h = jnp.zeros((n, half), F32)
    zeros_r = jnp.zeros((n, DA_DK - ROT_DIM), F32)
    cos64 = jnp.concatenate([cos, cos, ones], axis=1)
    sa64 = jnp.concatenate([-sin, zeros_h, zeros_r], axis=1)
    sb64 = jnp.concatenate([zeros_h, sin, zeros_r], axis=1)
    two = lambda a: jnp.concatenate([a, a], axis=1)
    return two(cos64), two(sa64), two(sb64)


def _trunk_layer(x, rope, layer, w, lam_init, seq, tm, past_kv, mem, hist, tag):
    t = x.shape[0]
    b = t // seq
    row2 = lambda a: a.reshape(1, -1)
    qkg = jnp.concatenate([jnp.tile(w['da_q_norm_g'], 2 * DA_HEADS),
                           jnp.tile(w['da_k_norm_g'], 2 * DA_HEADS)]).reshape(1, -1)
    gmg = jnp.tile(w['gm_norm_g'], GM_GROUPS).reshape(1, -1)
    gv_dtype = F32 if past_kv is not None else BF16
    q, k, v, u, gv = _mix_in(x, row2(w['norm_mix_g']), w['w_in'], qkg, gmg, rope, tm, gv_dtype,
                             f"mix_in_{tag}")
    lams = [row2(w[n]) for n in ('lambda_q1', 'lambda_k1', 'lambda_q2', 'lambda_k2')]
    subg = row2(w['da_subln_g'])
    q3 = q.reshape(b, seq, DA_QK_COLS)
    k3 = k.reshape(b, seq, DA_QK_COLS)
    v3 = v.reshape(b, seq, DA_WIDTH)
    if past_kv is None:
        o = _da_prompt(q3, k3, v3, lams, subg, lam_init, f"da_{tag}")
    else:
        o = _da_sample(q3, k3, v3, past_kv[0], past_kv[1], layer, lams, subg, lam_init, f"da_{tag}")
    chunk = min(seq, GM_CHUNK)
    ws = w['gm_w_s'][:, :chunk, :chunk]
    bs = jnp.broadcast_to(w['gm_b'][:, :chunk, None], (GM_GROUPS, chunk, GM_CH))
    x = _mix_out(x, o.reshape(t, DA_WIDTH), u, gv, ws, bs, w['w_out'], tm, chunk, f"mix_out_{tag}")
    nb = max(1, tm // seq)
    mk, mv, mem_spec = mem
    x = _xattn(x, row2(w['norm_x_g']), w['wq_c'], row2(w['xq_norm_g']), mk, mv, mem_spec(tm, seq),
               w['wo_c'], tm, nb, f"xattn_{tag}")
    tiles_per_b = max(1, seq // tm)
    x, cst = _conv_ffn(x, row2(w['norm_ffn_g']), w['w_up'], w['conv_w'], row2(w['conv_b']),
                       w['w_down'], hist, tm, nb, tiles_per_b, f"conv_ffn_{tag}")
    return x, k, v, gv, cst


def kernel(x_prompt, x_sample, cache_da_k, cache_da_v, cache_mem_k, cache_mem_v, state_ffn_conv, mem_prompt, norm_mix_g, w_in, da_q_norm_g, da_k_norm_g, lambda_q1, lambda_k1, lambda_q2, lambda_k2, da_subln_g, gm_norm_g, gm_w_s, gm_b, w_out, norm_x_g, norm_mem_g, wq_c, wk_c, wv_c, wo_c, xq_norm_g, xk_norm_g, norm_ffn_g, w_up, conv_w, conv_b, w_down):
    bp, sp, _ = x_prompt.shape
    bs_, ss, _ = x_sample.shape
    depth = w_in.shape[0]
    past = cache_da_k.shape[2]
    tm_p = min(TOKEN_TILE, sp)
    tm_s = bs_ * ss
    assert sp % tm_p == 0 and sp % Q_TILE == 0 and sp % GM_CHUNK == 0
    assert ss <= GM_CHUNK and tm_s % 8 == 0 and ss >= CONV_W - 1

    rope_p = _rope_tables(jnp.arange(sp))
    rope_s = tuple(jnp.tile(a, (bs_, 1)) for a in _rope_tables(past + jnp.arange(ss)))
    cache_k = cache_da_k.reshape(depth, bs_, past, DA_QK_COLS)
    cache_v = cache_da_v.reshape(depth, bs_, past, DA_WIDTH)
    cmk = cache_mem_k.reshape(depth, bs_, MEM_LEN, D_MODEL)
    cmv = cache_mem_v.reshape(depth, bs_, MEM_LEN, D_MODEL)
    hist_p = jnp.zeros((bp, CONV_W - 1, D_FF), F32)
    mem_flat = mem_prompt.reshape(bp * MEM_LEN, D_MODEL)

    xp = x_prompt.reshape(bp * sp, D_MODEL)
    xs = x_sample.reshape(tm_s, D_MODEL)
    outs = {n: [] for n in ('dk_p', 'dv_p', 'mk_p', 'mv_p', 'fc_p', 'dk_s', 'dv_s', 'gv_s', 'fc_s')}
    for l in range(depth):
        lam_init = 0.8 - 0.6 * math.exp(-0.3 * l)
        w = dict(norm_mix_g=norm_mix_g[l], w_in=w_in[l].astype(BF16), da_q_norm_g=da_q_norm_g[l],
                 da_k_norm_g=da_k_norm_g[l], lambda_q1=lambda_q1[l], lambda_k1=lambda_k1[l],
                 lambda_q2=lambda_q2[l], lambda_k2=lambda_k2[l], da_subln_g=da_subln_g[l],
                 gm_norm_g=gm_norm_g[l], gm_w_s=gm_w_s[l], gm_b=gm_b[l], w_out=w_out[l].astype(BF16),
                 norm_x_g=norm_x_g[l], wq_c=wq_c[l].astype(BF16), wo_c=wo_c[l].astype(BF16),
                 xq_norm_g=xq_norm_g[l], norm_ffn_g=norm_ffn_g[l], w_up=w_up[l].astype(BF16),
                 conv_w=conv_w[l], conv_b=conv_b[l], w_down=w_down[l].astype(BF16))
        mk, mv = _mem_kv(mem_flat, norm_mem_g[l].reshape(1, -1), wk_c[l].astype(BF16),
                         wv_c[l].astype(BF16), xk_norm_g[l].reshape(1, -1), TOKEN_TILE, f"mem_kv_{l}")
        mk3 = mk.reshape(bp, MEM_LEN, D_MODEL)
        mv3 = mv.reshape(bp, MEM_LEN, D_MODEL)
        mem_spec_p = lambda tm, seq: pl.BlockSpec((1, MEM_LEN, D_MODEL),
                                                  lambda i: (i // (seq // tm), 0, 0))
        xp, k, v, _, cst = _trunk_layer(xp, rope_p, l, w, lam_init, sp, tm_p, None,
                                        (mk3, mv3, mem_spec_p), hist_p, f"p{l}")
        outs['dk_p'].append(k.reshape(bp, sp, DA_HEADS, 2, DA_DK))
        outs['dv_p'].append(v.reshape(bp, sp, DA_HEADS, DA_DV))
        outs['mk_p'].append(mk.reshape(bp, MEM_LEN, X_HEADS, X_DH))
        outs['mv_p'].append(mv.reshape(bp, MEM_LEN, X_HEADS, X_DH))
        outs['fc_p'].append(cst)
        mem_spec_s = lambda tm, seq, l=l: pl.BlockSpec((1, bs_, MEM_LEN, D_MODEL),
                                                       lambda i: (l, 0, 0, 0))
        xs, k, v, gv, cst = _trunk_layer(xs, rope_s, l, w, lam_init, ss, tm_s, (cache_k, cache_v),
                                         (cmk, cmv, mem_spec_s), state_ffn_conv[l], f"s{l}")
        outs['dk_s'].append(k.reshape(bs_, ss, DA_HEADS, 2, DA_DK))
        outs['dv_s'].append(v.reshape(bs_, ss, DA_HEADS, DA_DV))
        outs['gv_s'].append(gv.reshape(bs_, ss, GM_GROUPS, GM_CH))
        outs['fc_s'].append(cst)

    st = {n: jnp.stack(a) for n, a in outs.items()}
    return (xp.reshape(bp, sp, D_MODEL), xs.reshape(bs_, ss, D_MODEL), st['dk_p'], st['dv_p'],
            st['mk_p'], st['mv_p'], st['fc_p'], st['dk_s'], st['dv_s'], st['gv_s'], st['fc_s'])
```

```python
import functools
import math

import numpy as np
import jax
import jax.numpy as jnp
from jax import lax
from jax.experimental import pallas as pl
from jax.experimental.pallas import tpu as pltpu

F32 = jnp.float32
BF16 = jnp.bfloat16

D_MODEL = 1024
CHUNK = 64
DA_HEADS = 4
DA_DK = 64
DA_DV = 128
DA_QK_COLS = DA_HEADS * 2 * DA_DK
DA_WIDTH = DA_HEADS * DA_DV
ROT_DIM = 16
ROT_HALF = ROT_DIM // 2
ROPE_THETA = 500000.0
GM_GROUPS = 4
GM_CH = 128
GM_WIDTH = GM_GROUPS * GM_CH
GM_CHUNK = 128
MEM_LEN = 256
X_HEADS = 4
X_DH = 256
D_FF = 2816
CONV_W = 3
EPS = 1e-6
IN_COLS = 2 * DA_QK_COLS + DA_WIDTH + 2 * GM_WIDTH
MAIN_COLS = IN_COLS - DA_QK_COLS
NEG = float(np.finfo(np.float32).min)
Q_SCALE = (DA_DK ** -0.5) * math.log2(math.e)

LANES = 128
SUBLANES = 8
VMEM_LIMIT_BYTES = 56 * 1024 * 1024
FF_CHUNK = 256
TOKEN_TILE = 512
Q_TILE = 256

_NT = (((1,), (1,)), ((), ()))


def _params(sem):
    return pltpu.CompilerParams(dimension_semantics=sem, vmem_limit_bytes=VMEM_LIMIT_BYTES)


def _const_spec(shape):
    n = len(shape)
    return pl.BlockSpec(shape, lambda *_: (0,) * n)


_ANY = pl.BlockSpec(memory_space=pl.ANY)


def _rms(x, g):
    return (x * lax.rsqrt(jnp.mean(x * x, axis=-1, keepdims=True) + EPS)) * g


def _gelu(x):
    return x * (0.5 * (1.0 + jnp.tanh(0.7978845608028654 * (x + 0.044715 * (x * x * x)))))


def _qk_slab(t, gain, cosv, sa, sb):
    lo = lax.broadcasted_iota(jnp.int32, (1, LANES), 1) < DA_DK
    sq = t * t
    s_lo = jnp.sum(jnp.where(lo, sq, 0.0), axis=-1, keepdims=True)
    s_hi = jnp.sum(jnp.where(lo, 0.0, sq), axis=-1, keepdims=True)
    ms = jnp.where(lo, s_lo, s_hi) * (1.0 / DA_DK)
    y = (t * lax.rsqrt(ms + EPS)) * gain
    return y * cosv + pltpu.roll(y, LANES - ROT_HALF, 1) * sa + pltpu.roll(y, ROT_HALF, 1) * sb


def _gmlp_cols(z, c0, gmg_ref, u_ref, gv_ref):
    u_ref[...] = _gelu(z[:, c0:c0 + GM_WIDTH]).astype(u_ref.dtype)
    c1 = c0 + GM_WIDTH
    for s in range(GM_GROUPS):
        cols = slice(s * GM_CH, (s + 1) * GM_CH)
        t = _gelu(z[:, c1 + s * GM_CH:c1 + (s + 1) * GM_CH])
        gv_ref[:, cols] = _rms(t, gmg_ref[:, cols]).astype(gv_ref.dtype)


def _mix_in_sample_body(x_ref, g_ref, w_ref, qg_ref, kg_ref, gmg_ref, cos_ref, sa_ref, sb_ref,
                        q_ref, k_ref, v_ref, u_ref, gv_ref):
    h = _rms(x_ref[...], g_ref[...]).astype(BF16)
    z = jnp.dot(h, w_ref[...], preferred_element_type=F32)
    cosv, sa, sb = cos_ref[...], sa_ref[...], sb_ref[...]
    for s in range(DA_HEADS):
        cols = slice(s * LANES, (s + 1) * LANES)
        q_ref[:, cols] = (_qk_slab(z[:, cols], qg_ref[...], cosv, sa, sb) * Q_SCALE).astype(q_ref.dtype)
        kc = slice(DA_QK_COLS + s * LANES, DA_QK_COLS + (s + 1) * LANES)
        k_ref[:, cols] = _qk_slab(z[:, kc], kg_ref[...], cosv, sa, sb)
    c2 = 2 * DA_QK_COLS
    v_ref[...] = z[:, c2:c2 + DA_WIDTH]
    _gmlp_cols(z, c2 + DA_WIDTH, gmg_ref, u_ref, gv_ref)


def _mix_in_sample(x, g, w, qg, kg, gmg, rope, name):
    t = x.shape[0]
    row = pl.BlockSpec((t, DA_QK_COLS), lambda i: (0, 0))
    rope_spec = _const_spec((t, LANES))
    return pl.pallas_call(
        _mix_in_sample_body,
        grid=(1,),
        in_specs=[_const_spec((t, D_MODEL)), _const_spec((1, D_MODEL)), _const_spec((D_MODEL, IN_COLS)),
                  _const_spec((1, LANES)), _const_spec((1, LANES)), _const_spec((1, GM_WIDTH)),
                  rope_spec, rope_spec, rope_spec],
        out_specs=[row] * 5,
        out_shape=[jax.ShapeDtypeStruct((t, DA_QK_COLS), BF16),
                   jax.ShapeDtypeStruct((t, DA_QK_COLS), F32),
                   jax.ShapeDtypeStruct((t, DA_WIDTH), F32),
                   jax.ShapeDtypeStruct((t, GM_WIDTH), BF16),
                   jax.ShapeDtypeStruct((t, GM_WIDTH), F32)],
        compiler_params=_params(("arbitrary",)),
        name=name,
    )(x, g, w, qg, kg, gmg, *rope)


def _mix_in_prompt_body(x_ref, g_ref, w_ref, wk_ref, qg_ref, kg_ref, gmg_ref, cos_ref, sa_ref, sb_ref,
                        cosk_ref, sink_ref, kt_in, v_in, q_ref, kt_ref, v_ref, vb_ref, u_ref, gv_ref):
    del kt_in, v_in
    h = _rms(x_ref[...], g_ref[...]).astype(BF16)
    z = jnp.dot(h, w_ref[...], preferred_element_type=F32)
    zk = lax.dot_general(wk_ref[...], h, _NT, preferred_element_type=F32)
    cosv, sa, sb = cos_ref[...], sa_ref[...], sb_ref[...]
    for s in range(DA_HEADS):
        cols = slice(s * LANES, (s + 1) * LANES)
        q_ref[:, cols] = (_qk_slab(z[:, cols], qg_ref[...], cosv, sa, sb) * Q_SCALE).astype(q_ref.dtype)
        v_ref[:, s, :] = z[:, DA_QK_COLS + s * DA_DV:DA_QK_COLS + (s + 1) * DA_DV]
    vb_ref[...] = z[:, DA_QK_COLS:DA_QK_COLS + DA_WIDTH].astype(vb_ref.dtype)
    ck, sk = cosk_ref[...], sink_ref[...]
    for grp in range(2 * DA_HEADS):
        r0 = grp * DA_DK
        t = zk[r0:r0 + DA_DK, :]
        inv = lax.rsqrt(jnp.sum(t * t, axis=0, keepdims=True) * (1.0 / DA_DK) + EPS)
        y = (t * inv) * kg_ref[...]
        ya, yb = y[:ROT_HALF], y[ROT_HALF:ROT_DIM]
        kt_ref[r0:r0 + ROT_HALF, :] = ya * ck - yb * sk
        kt_ref[r0 + ROT_HALF:r0 + ROT_DIM, :] = yb * ck + ya * sk
        kt_ref[r0 + ROT_DIM:r0 + DA_DK, :] = y[ROT_DIM:]
    _gmlp_cols(z, DA_QK_COLS + DA_WIDTH, gmg_ref, u_ref, gv_ref)


def _mix_in_prompt(x, g, w, wk, qg, kg_col, gmg, rope, rope_k, kt_all, v_all, layer, seq, tm, name):
    t = x.shape[0]
    per_b = seq // tm
    row = lambda i: (i, 0)
    rope_spec = pl.BlockSpec((tm, LANES), lambda i: (i % per_b, 0))
    ropek_spec = pl.BlockSpec((SUBLANES, tm), lambda i: (0, i % per_b))
    half = pl.BlockSpec((tm, DA_QK_COLS), row)
    kt_spec = pl.BlockSpec((None, None, DA_QK_COLS, tm), lambda i: (layer, i // per_b, 0, i % per_b))
    v_spec = pl.BlockSpec((None, None, tm, DA_HEADS, DA_DV), lambda i: (layer, i // per_b, i % per_b, 0, 0))
    return pl.pallas_call(
        _mix_in_prompt_body,
        grid=(t // tm,),
        in_specs=[pl.BlockSpec((tm, D_MODEL), row), _const_spec((1, D_MODEL)),
                  _const_spec((D_MODEL, MAIN_COLS)), _const_spec((DA_QK_COLS, D_MODEL)),
                  _const_spec((1, LANES)), _const_spec((DA_DK, tm)), _const_spec((1, GM_WIDTH)),
                  rope_spec, rope_spec, rope_spec, ropek_spec, ropek_spec, _ANY, _ANY],
        out_specs=[half, kt_spec, v_spec, half, half, half],
        out_shape=[jax.ShapeDtypeStruct((t, DA_QK_COLS), BF16),
                   jax.ShapeDtypeStruct(kt_all.shape, F32),
                   jax.ShapeDtypeStruct(v_all.shape, F32),
                   jax.ShapeDtypeStruct((t, DA_WIDTH), BF16),
                   jax.ShapeDtypeStruct((t, GM_WIDTH), BF16),
                   jax.ShapeDtypeStruct((t, GM_WIDTH), BF16)],
        input_output_aliases={12: 1, 13: 2},
        compiler_params=_params(("parallel",)),
        name=name,
    )(x, g, w, wk, qg, kg_col, gmg, *rope, *rope_k, kt_all, v_all)


def _lambda(lq1, lk1, lq2, lk2, lam_init):
    a = jnp.sum(lq1[...] * lk1[...], axis=-1, keepdims=True)
    b = jnp.sum(lq2[...] * lk2[...], axis=-1, keepdims=True)
    return jnp.exp(a) - jnp.exp(b) + lam_init


def _subln(o, g, lam_init):
    return _rms(o, g) * (1.0 - lam_init)


def _split_maps(q):
    lo = lax.broadcasted_iota(jnp.int32, (1, LANES), 1) < DA_DK
    zero = jnp.zeros_like(q)
    return jnp.where(lo, q, zero), jnp.where(lo, zero, q)


def _da_prompt_body(lq1, lk1, lq2, lk2, subg_ref, q_ref, kt_ref, v_ref, o_ref, *, lam_init, bq):
    seq = q_ref.shape[1]
    lam = _lambda(lq1, lk1, lq2, lk2, lam_init)
    qm = _split_maps(q_ref[0])
    kt = kt_ref[...].astype(BF16)
    va = v_ref[0]
    mask = ((lax.broadcasted_iota(jnp.int32, (bq, bq), 1) // CHUNK)
            <= (lax.broadcasted_iota(jnp.int32, (bq, bq), 0) // CHUNK))
    for qi in range(seq // bq):
        lo, hi = qi * bq, (qi + 1) * bq
        probs = []
        for m in range(2):
            q = qm[m][lo:hi]
            sd = jnp.where(mask, jnp.dot(q, kt[:, lo:hi], preferred_element_type=F32), NEG)
            mx = jnp.max(sd, axis=-1, keepdims=True)
            if qi > 0:
                so = jnp.dot(q, kt[:, :lo], preferred_element_type=F32)
                mx = jnp.maximum(mx, jnp.max(so, axis=-1, keepdims=True))
            pd = jnp.exp2(sd - mx)
            den = jnp.sum(pd, axis=-1, keepdims=True)
            po = None
            if qi > 0:
                po = jnp.exp2(so - mx)
                den = den + jnp.sum(po, axis=-1, keepdims=True)
            probs.append((pd, po, den))
        c1 = 1.0 / probs[0][2]
        r = lam * probs[0][2] / probs[1][2]
        o = jnp.dot((probs[0][0] - r * probs[1][0]).astype(BF16), va[lo:hi], preferred_element_type=F32)
        if qi > 0:
            o = o + jnp.dot((probs[0][1] - r * probs[1][1]).astype(BF16), va[:lo],
                            preferred_element_type=F32)
        o_ref[0, lo:hi, :] = _subln(o * c1, subg_ref[...], lam_init).astype(o_ref.dtype)


def _da_prompt(q, kt_all, vb, layer, lams, subg, lam_init, name):
    b, s, _ = q.shape
    lam_spec = _const_spec((1, DA_DK))
    q_spec = pl.BlockSpec((1, s, LANES), lambda bi, h: (bi, 0, h))
    kt_spec = pl.BlockSpec((None, None, 2 * DA_DK, s), lambda bi, h: (layer, bi, h, 0))
    return pl.pallas_call(
        functools.partial(_da_prompt_body, lam_init=lam_init, bq=Q_TILE),
        grid=(b, DA_HEADS),
        in_specs=[lam_spec, lam_spec, lam_spec, lam_spec, _const_spec((1, DA_DV)),
                  q_spec, kt_spec, q_spec],
        out_specs=q_spec,
        out_shape=jax.ShapeDtypeStruct((b, s, DA_WIDTH), BF16),
        compiler_params=_params(("parallel", "parallel")),
        name=name,
    )(*lams, subg, q, kt_all, vb)


def _da_sample_body(lq1, lk1, lq2, lk2, subg_ref, q_ref, ktp_ref, kn_ref, vp_ref, vn_ref, o_ref,
                    *, lam_init, past, sq):
    lam = _lambda(lq1, lk1, lq2, lk2, lam_init)
    q_chunk = (past + lax.broadcasted_iota(jnp.int32, (sq, 1), 0)) // CHUNK
    mask_p = (lax.broadcasted_iota(jnp.int32, (1, past), 1) // CHUNK) <= q_chunk
    mask_n = ((past + lax.broadcasted_iota(jnp.int32, (1, sq), 1)) // CHUNK) <= q_chunk
    for h in range(DA_HEADS):
        cols = slice(h * LANES, (h + 1) * LANES)
        qm = _split_maps(q_ref[0, :, cols])
        ktp = ktp_ref[cols, :].astype(BF16)
        kn = kn_ref[0, :, cols].astype(BF16)
        probs = []
        for m in range(2):
            sp = jnp.where(mask_p, jnp.dot(qm[m], ktp, preferred_element_type=F32), NEG)
            sn = jnp.where(mask_n, lax.dot_general(qm[m], kn, _NT, preferred_element_type=F32), NEG)
            mx = jnp.maximum(jnp.max(sp, axis=-1, keepdims=True), jnp.max(sn, axis=-1, keepdims=True))
            pp = jnp.exp2(sp - mx)
            pn = jnp.exp2(sn - mx)
            inv = 1.0 / (jnp.sum(pp, axis=-1, keepdims=True) + jnp.sum(pn, axis=-1, keepdims=True))
            probs.append((pp * inv, pn * inv))
        ap = (probs[0][0] - lam * probs[1][0]).astype(BF16)
        an = (probs[0][1] - lam * probs[1][1]).astype(BF16)
        o = (jnp.dot(ap, vp_ref[:, h, :].astype(BF16), preferred_element_type=F32)
             + jnp.dot(an, vn_ref[0, :, cols].astype(BF16), preferred_element_type=F32))
        o_ref[0, :, cols] = _subln(o, subg_ref[...], lam_init).astype(o_ref.dtype)


def _da_sample(q, k_new, v_new, cache_kt, cache_v, layer, lams, subg, lam_init, name):
    b, sq, _ = q.shape
    past = cache_kt.shape[3]
    lam_spec = _const_spec((1, DA_DK))
    new_spec = pl.BlockSpec((1, sq, DA_QK_COLS), lambda bi: (bi, 0, 0))
    ktp_spec = pl.BlockSpec((None, None, DA_QK_COLS, past), lambda bi: (layer, bi, 0, 0))
    vp_spec = pl.BlockSpec((None, None, past, DA_HEADS, DA_DV), lambda bi: (layer, bi, 0, 0, 0))
    return pl.pallas_call(
        functools.partial(_da_sample_body, lam_init=lam_init, past=past, sq=sq),
        grid=(b,),
        in_specs=[lam_spec, lam_spec, lam_spec, lam_spec, _const_spec((1, DA_DV)),
                  new_spec, ktp_spec, new_spec, vp_spec, new_spec],
        out_specs=new_spec,
        out_shape=jax.ShapeDtypeStruct((b, sq, DA_WIDTH), BF16),
        compiler_params=_params(("parallel",)),
        name=name,
    )(*lams, subg, q, cache_kt, k_new, cache_v, v_new)


def _mix_out_body(x_ref, o_ref, u_ref, gv_ref, ws_ref, bs_ref, wout_ref, xo_ref, cat_ref, *, chunk):
    tm = x_ref.shape[0]
    tri = (lax.broadcasted_iota(jnp.int32, (chunk, chunk), 1)
           <= lax.broadcasted_iota(jnp.int32, (chunk, chunk), 0))
    cat_ref[:, :DA_WIDTH] = o_ref[...]
    for g in range(GM_GROUPS):
        cols = slice(g * GM_CH, (g + 1) * GM_CH)
        w = jnp.where(tri, ws_ref[g], 0.0).astype(BF16)
        bias = bs_ref[g]
        for c in range(tm // chunk):
            rows = slice(c * chunk, (c + 1) * chunk)
            s = jnp.dot(w, gv_ref[rows, cols].astype(BF16), preferred_element_type=F32) + bias
            cat_ref[rows, DA_WIDTH + g * GM_CH:DA_WIDTH + (g + 1) * GM_CH] = (
                u_ref[rows, cols].astype(F32) * s).astype(BF16)
    xo_ref[...] = x_ref[...] + jnp.dot(cat_ref[...], wout_ref[...], preferred_element_type=F32)


def _mix_out(x, o, u, gv, ws, bs, wout, tm, chunk, name):
    t = x.shape[0]
    row = lambda i: (i, 0)
    half = pl.BlockSpec((tm, DA_WIDTH), row)
    return pl.pallas_call(
        functools.partial(_mix_out_body, chunk=chunk),
        grid=(t // tm,),
        in_specs=[pl.BlockSpec((tm, D_MODEL), row), half, half, half,
                  _const_spec((GM_GROUPS, chunk, chunk)), _const_spec((GM_GROUPS, chunk, GM_CH)),
                  _const_spec((D_MODEL, D_MODEL))],
        out_specs=pl.BlockSpec((tm, D_MODEL), row),
        out_shape=jax.ShapeDtypeStruct((t, D_MODEL), F32),
        scratch_shapes=[pltpu.VMEM((tm, D_MODEL), BF16)],
        compiler_params=_params(("parallel",)),
        name=name,
    )(x, o, u, gv, ws, bs, wout)


def _mem_kv_body(mem_ref, g_ref, wk_ref, wv_ref, kg_ref, k_in, v_in, k_ref, v_ref, kb_ref, vb_ref, *, nb):
    del k_in, v_in
    m = _rms(mem_ref[...], g_ref[...]).astype(BF16)
    k = jnp.dot(m, wk_ref[...], preferred_element_type=F32)
    v = jnp.dot(m, wv_ref[...], preferred_element_type=F32)
    vb_ref[...] = v.astype(vb_ref.dtype)
    for h in range(X_HEADS):
        cols = slice(h * X_DH, (h + 1) * X_DH)
        kn = _rms(k[:, cols], kg_ref[...])
        kb_ref[:, cols] = kn.astype(kb_ref.dtype)
        for bi in range(nb):
            rows = slice(bi * MEM_LEN, (bi + 1) * MEM_LEN)
            k_ref[bi, :, h, :] = kn[rows]
            v_ref[bi, :, h, :] = v[rows, cols]


def _mem_kv(mem, g, wk, wv, kg, mk_all, mv_all, layer, tm, name):
    t = mem.shape[0]
    nb = tm // MEM_LEN
    blk = pl.BlockSpec((tm, D_MODEL), lambda i: (i, 0))
    out_spec = pl.BlockSpec((None, nb, MEM_LEN, X_HEADS, X_DH), lambda i: (layer, i, 0, 0, 0))
    return pl.pallas_call(
        functools.partial(_mem_kv_body, nb=nb),
        grid=(t // tm,),
        in_specs=[blk, _const_spec((1, D_MODEL)), _const_spec((D_MODEL, D_MODEL)),
                  _const_spec((D_MODEL, D_MODEL)), _const_spec((1, X_DH)), _ANY, _ANY],
        out_specs=[out_spec, out_spec, blk, blk],
        out_shape=[jax.ShapeDtypeStruct(mk_all.shape, F32), jax.ShapeDtypeStruct(mv_all.shape, F32),
                   jax.ShapeDtypeStruct((t, D_MODEL), BF16), jax.ShapeDtypeStruct((t, D_MODEL), BF16)],
        input_output_aliases={5: 0, 6: 1},
        compiler_params=_params(("parallel",)),
        name=name,
    )(mem, g, wk, wv, kg, mk_all, mv_all)


def _xattn_body(x_ref, g_ref, wq_ref, qg_ref, mk_ref, mv_ref, wo_ref, xo_ref, att_ref):
    x = x_ref[...]
    h = _rms(x, g_ref[...]).astype(BF16)
    q = jnp.dot(h, wq_ref[...], preferred_element_type=F32)
    flat = len(mk_ref.shape) == 2
    for hh in range(X_HEADS):
        cols = slice(hh * X_DH, (hh + 1) * X_DH)
        mk = mk_ref[:, cols] if flat else mk_ref[:, hh, :]
        mv = mv_ref[:, cols] if flat else mv_ref[:, hh, :]
        qn = (_rms(q[:, cols], qg_ref[...]) * (X_DH ** -0.5)).astype(BF16)
        s = lax.dot_general(qn, mk.astype(BF16), _NT, preferred_element_type=F32)
        p = jnp.exp(s - jnp.max(s, axis=-1, keepdims=True))
        p = p * (1.0 / jnp.sum(p, axis=-1, keepdims=True))
        att_ref[:, cols] = jnp.dot(p.astype(BF16), mv.astype(BF16),
                                   preferred_element_type=F32).astype(BF16)
    xo_ref[...] = x + jnp.dot(att_ref[...], wo_ref[...], preferred_element_type=F32)


def _xattn(x, g, wq, qg, mk, mv, mem_spec, wo, tm, name):
    t = x.shape[0]
    blk = pl.BlockSpec((tm, D_MODEL), lambda i: (i, 0))
    return pl.pallas_call(
        _xattn_body,
        grid=(t // tm,),
        in_specs=[blk, _const_spec((1, D_MODEL)), _const_spec((D_MODEL, D_MODEL)),
                  _const_spec((1, X_DH)), mem_spec, mem_spec, _const_spec((D_MODEL, D_MODEL))],
        out_specs=blk,
        out_shape=jax.ShapeDtypeStruct((t, D_MODEL), F32),
        scratch_shapes=[pltpu.VMEM((tm, D_MODEL), BF16)],
        compiler_params=_params(("parallel",)),
        name=name,
    )(x, g, wq, qg, mk, mv, wo)


def _conv_ffn_body(x_ref, g_ref, wup_ref, cw_ref, cb_ref, wdn_ref, hist_ref, xo_ref, cst_ref,
                   gp_ref, carry_ref, act_ref, *, nb, rows_per_b, tiles_per_b):
    i = pl.program_id(0)
    r = rows_per_b
    x = x_ref[...]
    h = _rms(x, g_ref[...]).astype(BF16)
    first = (i % tiles_per_b) == 0
    for c in range(D_FF // FF_CHUNK):
        cs = slice(c * FF_CHUNK, (c + 1) * FF_CHUNK)
        g = jnp.dot(h, wup_ref[:, cs], preferred_element_type=F32)
        up = jnp.dot(h, wup_ref[:, D_FF + c * FF_CHUNK:D_FF + (c + 1) * FF_CHUNK],
                     preferred_element_type=F32)
        for bi in range(nb):
            rows = slice(bi * r, (bi + 1) * r)
            if tiles_per_b == 1:
                prev = hist_ref[bi, :, cs]
            else:
                prev = jnp.where(first, hist_ref[bi, :, cs], carry_ref[:, cs])
            gp_ref[6:8, :] = prev
            gp_ref[8:8 + r, :] = g[rows]
            conv = (cb_ref[:, cs] + cw_ref[0:1, cs] * gp_ref[6:6 + r, :]
                    + cw_ref[1:2, cs] * gp_ref[7:7 + r, :] + cw_ref[2:3, cs] * g[rows])
            act = conv * (1.0 / (1.0 + jnp.exp(-conv))) * up[rows]
            act_ref[rows, cs] = act.astype(BF16)
            last2 = gp_ref[6 + r:8 + r, :]
            cst_ref[bi, :, cs] = last2
            if tiles_per_b > 1:
                carry_ref[:, cs] = last2
    xo_ref[...] = x + jnp.dot(act_ref[...], wdn_ref[...], preferred_element_type=F32)


def _conv_ffn(x, g, wup, cw, cb, wdn, hist, tm, nb, tiles_per_b, name):
    t = x.shape[0]
    row = lambda i: (i, 0)
    blk = pl.BlockSpec((tm, D_MODEL), row)
    hist_spec = pl.BlockSpec((nb, CONV_W - 1, D_FF), lambda i: (i // tiles_per_b, 0, 0))
    return pl.pallas_call(
        functools.partial(_conv_ffn_body, nb=nb, rows_per_b=tm // nb, tiles_per_b=tiles_per_b),
        grid=(t // tm,),
        in_specs=[blk, _const_spec((1, D_MODEL)), _const_spec((D_MODEL, 2 * D_FF)),
                  _const_spec((CONV_W, D_FF)), _const_spec((1, D_FF)), _const_spec((D_FF, D_MODEL)),
                  hist_spec],
        out_specs=[blk, hist_spec],
        out_shape=[jax.ShapeDtypeStruct((t, D_MODEL), F32),
                   jax.ShapeDtypeStruct(hist.shape, F32)],
        scratch_shapes=[pltpu.VMEM((tm // nb + 8, FF_CHUNK), F32),
                        pltpu.VMEM((CONV_W - 1, D_FF), F32),
                        pltpu.VMEM((tm, D_FF), BF16)],
        compiler_params=_params(("arbitrary",)),
        name=name,
    )(x, g, wup, cw, cb, wdn, hist)


def _rope_angles(pos):
    inv = ROPE_THETA ** (-jnp.arange(ROT_HALF, dtype=F32) / ROT_HALF)
    ang = pos.astype(F32)[:, None] * inv[None, :]
    return jnp.cos(ang), jnp.sin(ang)


def _rope_tables(pos):
    cos, sin = _rope_angles(pos)
    n = pos.shape[0]
    ones = jnp.ones((n, DA_DK - ROT_DIM), F32)
    zeros_h = jnp.zeros((n, ROT_HALF), F32)
    zeros_r = jnp.zeros((n, DA_DK - ROT_DIM), F32)
    cos64 = jnp.concatenate([cos, cos, ones], axis=1)
    sa64 = jnp.concatenate([-sin, zeros_h, zeros_r], axis=1)
    sb64 = jnp.concatenate([zeros_h, sin, zeros_r], axis=1)
    two = lambda a: jnp.concatenate([a, a], axis=1)
    return two(cos64), two(sa64), two(sb64)


def _post_attention(x, o, u, gv, w, mk, mv, mem_spec, hist, seq, tm, tag):
    t = x.shape[0]
    row2 = lambda a: a.reshape(1, -1)
    chunk = min(seq, GM_CHUNK)
    ws = w['gm_w_s'][:, :chunk, :chunk]
    bs = jnp.broadcast_to(w['gm_b'][:, :chunk, None], (GM_GROUPS, chunk, GM_CH))
    x = _mix_out(x, o.reshape(t, DA_WIDTH), u, gv, ws, bs, w['w_out'], tm, chunk, f"mix_out_{tag}")
    xa_tm = min(tm, seq)
    x = _xattn(x, row2(w['norm_x_g']), w['wq_c'], row2(w['xq_norm_g']), mk, mv, mem_spec,
               w['wo_c'], xa_tm, f"xattn_{tag}")
    nb = max(1, tm // seq)
    tiles_per_b = max(1, seq // tm)
    return _conv_ffn(x, row2(w['norm_ffn_g']), w['w_up'], w['conv_w'], row2(w['conv_b']),
                     w['w_down'], hist, tm, nb, tiles_per_b, f"conv_ffn_{tag}")


def kernel(x_prompt, x_sample, cache_da_k, cache_da_v, cache_mem_k, cache_mem_v, state_ffn_conv, mem_prompt, norm_mix_g, w_in, da_q_norm_g, da_k_norm_g, lambda_q1, lambda_k1, lambda_q2, lambda_k2, da_subln_g, gm_norm_g, gm_w_s, gm_b, w_out, norm_x_g, norm_mem_g, wq_c, wk_c, wv_c, wo_c, xq_norm_g, xk_norm_g, norm_ffn_g, w_up, conv_w, conv_b, w_down):
    bp, sp, _ = x_prompt.shape
    bs_, ss, _ = x_sample.shape
    depth = w_in.shape[0]
    past = cache_da_k.shape[2]
    tm_p = min(TOKEN_TILE, sp)
    tm_s = bs_ * ss
    tm_mem = min(TOKEN_TILE, bp * MEM_LEN)
    assert sp % tm_p == 0 and sp % Q_TILE == 0 and sp % GM_CHUNK == 0 and tm_p % LANES == 0
    assert ss <= GM_CHUNK and tm_s % SUBLANES == 0 and ss >= CONV_W - 1 and ss % SUBLANES == 0
    assert (bp * MEM_LEN) % tm_mem == 0 and tm_mem % MEM_LEN == 0

    pos_p = jnp.arange(sp)
    rope_p = _rope_tables(pos_p)
    cos_p, sin_p = _rope_angles(pos_p)
    rope_kp = (cos_p.T, sin_p.T)
    rope_s = tuple(jnp.tile(a, (bs_, 1)) for a in _rope_tables(past + jnp.arange(ss)))
    cache_kt = jnp.transpose(cache_da_k, (0, 1, 3, 4, 5, 2)).reshape(depth, bs_, DA_QK_COLS, past)
    hist_p = jnp.zeros((bp, CONV_W - 1, D_FF), F32)
    mem_flat = mem_prompt.reshape(bp * MEM_LEN, D_MODEL)

    kt_all = jnp.zeros((depth, bp, DA_QK_COLS, sp), F32)
    v_all = jnp.zeros((depth, bp, sp, DA_HEADS, DA_DV), F32)
    mk_all = jnp.zeros((depth, bp, MEM_LEN, X_HEADS, X_DH), F32)
    mv_all = jnp.zeros((depth, bp, MEM_LEN, X_HEADS, X_DH), F32)

    xp = x_prompt.reshape(bp * sp, D_MODEL)
    xs = x_sample.reshape(tm_s, D_MODEL)
    outs = {n: [] for n in ('fc_p', 'dk_s', 'dv_s', 'gv_s', 'fc_s')}
    row2 = lambda a: a.reshape(1, -1)
    for l in range(depth):
        lam_init = 0.8 - 0.6 * math.exp(-0.3 * l)
        w = dict(gm_w_s=gm_w_s[l], gm_b=gm_b[l], w_out=w_out[l].astype(BF16),
                 norm_x_g=norm_x_g[l], wq_c=wq_c[l].astype(BF16), wo_c=wo_c[l].astype(BF16),
                 xq_norm_g=xq_norm_g[l], norm_ffn_g=norm_ffn_g[l], w_up=w_up[l].astype(BF16),
                 conv_w=conv_w[l], conv_b=conv_b[l], w_down=w_down[l].astype(BF16))
        w_in_l = w_in[l].astype(BF16)
        w_main = jnp.concatenate([w_in_l[:, :DA_QK_COLS], w_in_l[:, 2 * DA_QK_COLS:]], axis=1)
        w_kt = w_in_l[:, DA_QK_COLS:2 * DA_QK_COLS].T
        qg = jnp.tile(da_q_norm_g[l], 2).reshape(1, -1)
        kg = jnp.tile(da_k_norm_g[l], 2).reshape(1, -1)
        kg_col = jnp.broadcast_to(da_k_norm_g[l][:, None], (DA_DK, tm_p))
        gmg = jnp.tile(gm_norm_g[l], GM_GROUPS).reshape(1, -1)
        lams = [row2(a[l]) for a in (lambda_q1, lambda_k1, lambda_q2, lambda_k2)]
        subg = row2(da_subln_g[l])
        g_mix = row2(norm_mix_g[l])

        mk_all, mv_all, mkb, mvb = _mem_kv(mem_flat, row2(norm_mem_g[l]), wk_c[l].astype(BF16),
                                           wv_c[l].astype(BF16), row2(xk_norm_g[l]), mk_all, mv_all, l,
                                           tm_mem, f"mem_kv_{l}")
        q, kt_all, v_all, vb, u, gv = _mix_in_prompt(xp, g_mix, w_main, w_kt, qg, kg_col, gmg, rope_p, rope_kp,
                                                     kt_all, v_all, l, sp, tm_p, f"mix_in_p{l}")
        o = _da_prompt(q.reshape(bp, sp, DA_QK_COLS), kt_all, vb.reshape(bp, sp, DA_WIDTH), l, lams, subg,
                       lam_init, f"da_p{l}")
        mem_spec_p = pl.BlockSpec((MEM_LEN, D_MODEL), lambda i: (i // (sp // tm_p), 0))
        xp, cst = _post_attention(xp, o, u, gv, w, mkb, mvb, mem_spec_p, hist_p, sp, tm_p, f"p{l}")
        outs['fc_p'].append(cst)

        q, k, v, u, gv = _mix_in_sample(xs, g_mix, w_in_l, qg, kg, gmg, rope_s, f"mix_in_s{l}")
        o = _da_sample(q.reshape(bs_, ss, DA_QK_COLS), k.reshape(bs_, ss, DA_QK_COLS),
                       v.reshape(bs_, ss, DA_WIDTH), cache_kt, cache_da_v, l, lams, subg, lam_init,
                       f"da_s{l}")
        mem_spec_s = pl.BlockSpec((None, None, MEM_LEN, X_HEADS, X_DH), lambda i, l=l: (l, i, 0, 0, 0))
        xs, cst = _post_attention(xs, o, u, gv, w, cache_mem_k, cache_mem_v, mem_spec_s, state_ffn_conv[l],
                                  ss, tm_s, f"s{l}")
        outs['dk_s'].append(k.reshape(bs_, ss, DA_HEADS, 2, DA_DK))
        outs['dv_s'].append(v.reshape(bs_, ss, DA_HEADS, DA_DV))
        outs['gv_s'].append(gv.reshape(bs_, ss, GM_GROUPS, GM_CH))
        outs['fc_s'].append(cst)

    st = {n: jnp.stack(a) for n, a in outs.items()}
    dk_p = jnp.transpose(kt_all.reshape(depth, bp, DA_HEADS, 2, DA_DK, sp), (0, 1, 5, 2, 3, 4))
    return (xp.reshape(bp, sp, D_MODEL), xs.reshape(bs_, ss, D_MODEL), dk_p, v_all, mk_all, mv_all,
            st['fc_p'], st['dk_s'], st['dv_s'], st['gv_s'], st['fc_s'])
```

```python
import functools
import math

import numpy as np
import jax
import jax.numpy as jnp
from jax import lax
from jax.experimental import pallas as pl
from jax.experimental.pallas import tpu as pltpu

F32 = jnp.float32
BF16 = jnp.bfloat16

D_MODEL = 1024
CHUNK = 64
DA_HEADS = 4
DA_DK = 64
DA_DV = 128
DA_QK_COLS = DA_HEADS * 2 * DA_DK
DA_WIDTH = DA_HEADS * DA_DV
ROT_DIM = 16
ROT_HALF = ROT_DIM // 2
ROPE_THETA = 500000.0
GM_GROUPS = 4
GM_CH = 128
GM_WIDTH = GM_GROUPS * GM_CH
GM_CHUNK = 128
MEM_LEN = 256
X_HEADS = 4
X_DH = 256
D_FF = 2816
CONV_W = 3
EPS = 1e-6
IN_COLS = 2 * DA_QK_COLS + DA_WIDTH + 2 * GM_WIDTH
MAIN_COLS = IN_COLS - DA_QK_COLS
NEG = float(np.finfo(np.float32).min)
Q_SCALE = (DA_DK ** -0.5) * math.log2(math.e)

LANES = 128
SUBLANES = 8
VMEM_LIMIT_BYTES = 56 * 1024 * 1024
FF_CHUNK = 256
TOKEN_TILE = 512
WIDE_TILE = 1024
Q_TILE = 256
SUB_ROWS = 512

_NT = (((1,), (1,)), ((), ()))


def _params(sem):
    return pltpu.CompilerParams(dimension_semantics=sem, vmem_limit_bytes=VMEM_LIMIT_BYTES)


def _const_spec(shape):
    n = len(shape)
    return pl.BlockSpec(shape, lambda *_: (0,) * n)


_ANY = pl.BlockSpec(memory_space=pl.ANY)


def _rms(x, g):
    return (x * lax.rsqrt(jnp.mean(x * x, axis=-1, keepdims=True) + EPS)) * g


def _gelu(x):
    return x * (0.5 * (1.0 + jnp.tanh(0.7978845608028654 * (x + 0.044715 * (x * x * x)))))


def _qk_slab(t, gain, cosv, sa, sb):
    lo = lax.broadcasted_iota(jnp.int32, (1, LANES), 1) < DA_DK
    sq = t * t
    s_lo = jnp.sum(jnp.where(lo, sq, 0.0), axis=-1, keepdims=True)
    s_hi = jnp.sum(jnp.where(lo, 0.0, sq), axis=-1, keepdims=True)
    ms = jnp.where(lo, s_lo, s_hi) * (1.0 / DA_DK)
    y = (t * lax.rsqrt(ms + EPS)) * gain
    return y * cosv + pltpu.roll(y, LANES - ROT_HALF, 1) * sa + pltpu.roll(y, ROT_HALF, 1) * sb


def _gmlp_cols(z, c0, gmg_ref, u_ref, gv_ref, rows=slice(None)):
    u_ref[rows, :] = _gelu(z[:, c0:c0 + GM_WIDTH]).astype(u_ref.dtype)
    c1 = c0 + GM_WIDTH
    for s in range(GM_GROUPS):
        cols = slice(s * GM_CH, (s + 1) * GM_CH)
        t = _gelu(z[:, c1 + s * GM_CH:c1 + (s + 1) * GM_CH])
        gv_ref[rows, cols] = _rms(t, gmg_ref[:, cols]).astype(gv_ref.dtype)


def _mix_in_sample_body(x_ref, g_ref, w_ref, qg_ref, kg_ref, gmg_ref, cos_ref, sa_ref, sb_ref,
                        q_ref, k_ref, v_ref, u_ref, gv_ref):
    h = _rms(x_ref[...], g_ref[...]).astype(BF16)
    z = jnp.dot(h, w_ref[...], preferred_element_type=F32)
    cosv, sa, sb = cos_ref[...], sa_ref[...], sb_ref[...]
    for s in range(DA_HEADS):
        cols = slice(s * LANES, (s + 1) * LANES)
        q_ref[:, cols] = (_qk_slab(z[:, cols], qg_ref[...], cosv, sa, sb) * Q_SCALE).astype(q_ref.dtype)
        kc = slice(DA_QK_COLS + s * LANES, DA_QK_COLS + (s + 1) * LANES)
        k_ref[:, cols] = _qk_slab(z[:, kc], kg_ref[...], cosv, sa, sb)
    c2 = 2 * DA_QK_COLS
    v_ref[...] = z[:, c2:c2 + DA_WIDTH]
    _gmlp_cols(z, c2 + DA_WIDTH, gmg_ref, u_ref, gv_ref)


def _mix_in_sample(x, g, w, qg, kg, gmg, rope, name):
    t = x.shape[0]
    row = pl.BlockSpec((t, DA_QK_COLS), lambda i: (0, 0))
    rope_spec = _const_spec((t, LANES))
    return pl.pallas_call(
        _mix_in_sample_body,
        grid=(1,),
        in_specs=[_const_spec((t, D_MODEL)), _const_spec((1, D_MODEL)), _const_spec((D_MODEL, IN_COLS)),
                  _const_spec((1, LANES)), _const_spec((1, LANES)), _const_spec((1, GM_WIDTH)),
                  rope_spec, rope_spec, rope_spec],
        out_specs=[row] * 5,
        out_shape=[jax.ShapeDtypeStruct((t, DA_QK_COLS), BF16),
                   jax.ShapeDtypeStruct((t, DA_QK_COLS), F32),
                   jax.ShapeDtypeStruct((t, DA_WIDTH), F32),
                   jax.ShapeDtypeStruct((t, GM_WIDTH), BF16),
                   jax.ShapeDtypeStruct((t, GM_WIDTH), F32)],
        compiler_params=_params(("arbitrary",)),
        name=name,
    )(x, g, w, qg, kg, gmg, *rope)


def _mix_in_prompt_body(x_ref, g_ref, w_ref, wk_ref, qg_ref, kg_ref, gmg_ref, cos_ref, sa_ref, sb_ref,
                        cosk_ref, sink_ref, kt_in, v_in, q_ref, kt_ref, v_ref, vb_ref, u_ref, gv_ref):
    del kt_in, v_in
    tm = x_ref.shape[0]

    def project(j):
        rows = slice(j * SUB_ROWS, (j + 1) * SUB_ROWS)
        h = _rms(x_ref[rows, :], g_ref[...]).astype(BF16)
        z = jnp.dot(h, w_ref[...], preferred_element_type=F32)
        zk = lax.dot_general(wk_ref[...], h, _NT, preferred_element_type=F32)
        return z, zk

    def finish(j, z, zk):
        rows = slice(j * SUB_ROWS, (j + 1) * SUB_ROWS)
        cosv, sa, sb = cos_ref[rows, :], sa_ref[rows, :], sb_ref[rows, :]
        for s in range(DA_HEADS):
            cols = slice(s * LANES, (s + 1) * LANES)
            q_ref[rows, cols] = (_qk_slab(z[:, cols], qg_ref[...], cosv, sa, sb) * Q_SCALE).astype(q_ref.dtype)
            v_ref[rows, s, :] = z[:, DA_QK_COLS + s * DA_DV:DA_QK_COLS + (s + 1) * DA_DV]
        vb_ref[rows, :] = z[:, DA_QK_COLS:DA_QK_COLS + DA_WIDTH].astype(vb_ref.dtype)
        ck, sk, kg = cosk_ref[:, rows], sink_ref[:, rows], kg_ref[...]
        for grp in range(2 * DA_HEADS):
            r0 = grp * DA_DK
            t = zk[r0:r0 + DA_DK, :]
            inv = lax.rsqrt(jnp.sum(t * t, axis=0, keepdims=True) * (1.0 / DA_DK) + EPS)
            y = (t * inv) * kg
            ya, yb = y[:ROT_HALF], y[ROT_HALF:ROT_DIM]
            kt_ref[r0:r0 + ROT_HALF, rows] = ya * ck - yb * sk
            kt_ref[r0 + ROT_HALF:r0 + ROT_DIM, rows] = yb * ck + ya * sk
            kt_ref[r0 + ROT_DIM:r0 + DA_DK, rows] = y[ROT_DIM:]
        _gmlp_cols(z, DA_QK_COLS + DA_WIDTH, gmg_ref, u_ref, gv_ref, rows)

    n_sub = tm // SUB_ROWS
    nxt = project(0)
    for j in range(n_sub):
        cur = nxt
        if j + 1 < n_sub:
            nxt = project(j + 1)
        finish(j, *cur)


def _mix_in_prompt(x, g, w, wk, qg, kg_col, gmg, rope, rope_k, kt_all, v_all, layer, seq, tm, name):
    t = x.shape[0]
    per_b = seq // tm
    row = lambda i: (i, 0)
    rope_spec = pl.BlockSpec((tm, LANES), lambda i: (i % per_b, 0))
    ropek_spec = pl.BlockSpec((SUBLANES, tm), lambda i: (0, i % per_b))
    half = pl.BlockSpec((tm, DA_QK_COLS), row)
    kt_spec = pl.BlockSpec((None, None, DA_QK_COLS, tm), lambda i: (layer, i // per_b, 0, i % per_b))
    v_spec = pl.BlockSpec((None, None, tm, DA_HEADS, DA_DV), lambda i: (layer, i // per_b, i % per_b, 0, 0))
    return pl.pallas_call(
        _mix_in_prompt_body,
        grid=(t // tm,),
        in_specs=[pl.BlockSpec((tm, D_MODEL), row), _const_spec((1, D_MODEL)),
                  _const_spec((D_MODEL, MAIN_COLS)), _const_spec((DA_QK_COLS, D_MODEL)),
                  _const_spec((1, LANES)), _const_spec((DA_DK, SUB_ROWS)), _const_spec((1, GM_WIDTH)),
                  rope_spec, rope_spec, rope_spec, ropek_spec, ropek_spec, _ANY, _ANY],
        out_specs=[half, kt_spec, v_spec, half, half, half],
        out_shape=[jax.ShapeDtypeStruct((t, DA_QK_COLS), BF16),
                   jax.ShapeDtypeStruct(kt_all.shape, F32),
                   jax.ShapeDtypeStruct(v_all.shape, F32),
                   jax.ShapeDtypeStruct((t, DA_WIDTH), BF16),
                   jax.ShapeDtypeStruct((t, GM_WIDTH), BF16),
                   jax.ShapeDtypeStruct((t, GM_WIDTH), BF16)],
        input_output_aliases={12: 1, 13: 2},
        compiler_params=_params(("parallel",)),
        name=name,
    )(x, g, w, wk, qg, kg_col, gmg, *rope, *rope_k, kt_all, v_all)


def _lambda(lq1, lk1, lq2, lk2, lam_init):
    a = jnp.sum(lq1[...] * lk1[...], axis=-1, keepdims=True)
    b = jnp.sum(lq2[...] * lk2[...], axis=-1, keepdims=True)
    return jnp.exp(a) - jnp.exp(b) + lam_init


def _subln(o, g, lam_init):
    return _rms(o, g) * (1.0 - lam_init)


def _split_maps(q):
    lo = lax.broadcasted_iota(jnp.int32, (1, LANES), 1) < DA_DK
    zero = jnp.zeros_like(q)
    return jnp.where(lo, q, zero), jnp.where(lo, zero, q)


def _da_prompt_body(lq1, lk1, lq2, lk2, subg_ref, q_ref, kt_ref, v_ref, o_ref, *, lam_init, bq):
    seq = q_ref.shape[1]
    lam = _lambda(lq1, lk1, lq2, lk2, lam_init)
    qm = _split_maps(q_ref[0])
    kt = kt_ref[...].astype(BF16)
    va = v_ref[0]
    mask = ((lax.broadcasted_iota(jnp.int32, (bq, bq), 1) // CHUNK)
            <= (lax.broadcasted_iota(jnp.int32, (bq, bq), 0) // CHUNK))
    def scores(qi):
        lo, hi = qi * bq, (qi + 1) * bq
        out = []
        for m in range(2):
            q = qm[m][lo:hi]
            sd = jnp.dot(q, kt[:, lo:hi], preferred_element_type=F32)
            so = jnp.dot(q, kt[:, :lo], preferred_element_type=F32) if qi > 0 else None
            out.append((sd, so))
        return out

    nq = seq // bq
    nxt = scores(0)
    for qi in range(nq):
        lo, hi = qi * bq, (qi + 1) * bq
        cur = nxt
        if qi + 1 < nq:
            nxt = scores(qi + 1)
        probs = []
        for m in range(2):
            sd, so = cur[m]
            sd = jnp.where(mask, sd, NEG)
            mx = jnp.max(sd, axis=-1, keepdims=True)
            if qi > 0:
                mx = jnp.maximum(mx, jnp.max(so, axis=-1, keepdims=True))
            pd = jnp.exp2(sd - mx)
            den = jnp.sum(pd, axis=-1, keepdims=True)
            po = None
            if qi > 0:
                po = jnp.exp2(so - mx)
                den = den + jnp.sum(po, axis=-1, keepdims=True)
            probs.append((pd, po, den))
        c1 = 1.0 / probs[0][2]
        r = lam * probs[0][2] / probs[1][2]
        o = jnp.dot((probs[0][0] - r * probs[1][0]).astype(BF16), va[lo:hi], preferred_element_type=F32)
        if qi > 0:
            o = o + jnp.dot((probs[0][1] - r * probs[1][1]).astype(BF16), va[:lo],
                            preferred_element_type=F32)
        o_ref[0, lo:hi, :] = _subln(o * c1, subg_ref[...], lam_init).astype(o_ref.dtype)


def _da_prompt(q, kt_all, vb, layer, lams, subg, lam_init, name):
    b, s, _ = q.shape
    lam_spec = _const_spec((1, DA_DK))
    q_spec = pl.BlockSpec((1, s, LANES), lambda bi, h: (bi, 0, h))
    kt_spec = pl.BlockSpec((None, None, 2 * DA_DK, s), lambda bi, h: (layer, bi, h, 0))
    return pl.pallas_call(
        functools.partial(_da_prompt_body, lam_init=lam_init, bq=Q_TILE),
        grid=(b, DA_HEADS),
        in_specs=[lam_spec, lam_spec, lam_spec, lam_spec, _const_spec((1, DA_DV)),
                  q_spec, kt_spec, q_spec],
        out_specs=q_spec,
        out_shape=jax.ShapeDtypeStruct((b, s, DA_WIDTH), BF16),
        compiler_params=_params(("parallel", "parallel")),
        name=name,
    )(*lams, subg, q, kt_all, vb)


def _da_sample_body(lq1, lk1, lq2, lk2, subg_ref, q_ref, ktp_ref, kn_ref, vp_ref, vn_ref, o_ref,
                    *, lam_init, past, sq):
    lam = _lambda(lq1, lk1, lq2, lk2, lam_init)
    q_chunk = (past + lax.broadcasted_iota(jnp.int32, (sq, 1), 0)) // CHUNK
    mask_p = (lax.broadcasted_iota(jnp.int32, (1, past), 1) // CHUNK) <= q_chunk
    mask_n = ((past + lax.broadcasted_iota(jnp.int32, (1, sq), 1)) // CHUNK) <= q_chunk
    for h in range(DA_HEADS):
        cols = slice(h * LANES, (h + 1) * LANES)
        qm = _split_maps(q_ref[0, :, cols])
        ktp = ktp_ref[cols, :].astype(BF16)
        kn = kn_ref[0, :, cols].astype(BF16)
        probs = []
        for m in range(2):
            sp = jnp.where(mask_p, jnp.dot(qm[m], ktp, preferred_element_type=F32), NEG)
            sn = jnp.where(mask_n, lax.dot_general(qm[m], kn, _NT, preferred_element_type=F32), NEG)
            mx = jnp.maximum(jnp.max(sp, axis=-1, keepdims=True), jnp.max(sn, axis=-1, keepdims=True))
            pp = jnp.exp2(sp - mx)
            pn = jnp.exp2(sn - mx)
            inv = 1.0 / (jnp.sum(pp, axis=-1, keepdims=True) + jnp.sum(pn, axis=-1, keepdims=True))
            probs.append((pp * inv, pn * inv))
        ap = (probs[0][0] - lam * probs[1][0]).astype(BF16)
        an = (probs[0][1] - lam * probs[1][1]).astype(BF16)
        o = (jnp.dot(ap, vp_ref[:, h, :].astype(BF16), preferred_element_type=F32)
             + jnp.dot(an, vn_ref[0, :, cols].astype(BF16), preferred_element_type=F32))
        o_ref[0, :, cols] = _subln(o, subg_ref[...], lam_init).astype(o_ref.dtype)


def _da_sample(q, k_new, v_new, cache_kt, cache_v, layer, lams, subg, lam_init, name):
    b, sq, _ = q.shape
    past = cache_kt.shape[3]
    lam_spec = _const_spec((1, DA_DK))
    new_spec = pl.BlockSpec((1, sq, DA_QK_COLS), lambda bi: (bi, 0, 0))
    ktp_spec = pl.BlockSpec((None, None, DA_QK_COLS, past), lambda bi: (layer, bi, 0, 0))
    vp_spec = pl.BlockSpec((None, None, past, DA_HEADS, DA_DV), lambda bi: (layer, bi, 0, 0, 0))
    return pl.pallas_call(
        functools.partial(_da_sample_body, lam_init=lam_init, past=past, sq=sq),
        grid=(b,),
        in_specs=[lam_spec, lam_spec, lam_spec, lam_spec, _const_spec((1, DA_DV)),
                  new_spec, ktp_spec, new_spec, vp_spec, new_spec],
        out_specs=new_spec,
        out_shape=jax.ShapeDtypeStruct((b, sq, DA_WIDTH), BF16),
        compiler_params=_params(("parallel",)),
        name=name,
    )(*lams, subg, q, cache_kt, k_new, cache_v, v_new)


def _mix_out_body(x_ref, o_ref, u_ref, gv_ref, ws_ref, bs_ref, wout_ref, xo_ref, cat_ref, *, chunk):
    tm = x_ref.shape[0]
    tri = (lax.broadcasted_iota(jnp.int32, (chunk, chunk), 1)
           <= lax.broadcasted_iota(jnp.int32, (chunk, chunk), 0))
    cat_ref[:, :DA_WIDTH] = o_ref[...]
    for g in range(GM_GROUPS):
        cols = slice(g * GM_CH, (g + 1) * GM_CH)
        w = jnp.where(tri, ws_ref[g], 0.0).astype(BF16)
        bias = bs_ref[g]
        for c in range(tm // chunk):
            rows = slice(c * chunk, (c + 1) * chunk)
            s = jnp.dot(w, gv_ref[rows, cols].astype(BF16), preferred_element_type=F32) + bias
            cat_ref[rows, DA_WIDTH + g * GM_CH:DA_WIDTH + (g + 1) * GM_CH] = (
                u_ref[rows, cols].astype(F32) * s).astype(BF16)
    xo_ref[...] = x_ref[...] + jnp.dot(cat_ref[...], wout_ref[...], preferred_element_type=F32)


def _mix_out(x, o, u, gv, ws, bs, wout, tm, chunk, name):
    t = x.shape[0]
    row = lambda i: (i, 0)
    half = pl.BlockSpec((tm, DA_WIDTH), row)
    return pl.pallas_call(
        functools.partial(_mix_out_body, chunk=chunk),
        grid=(t // tm,),
        in_specs=[pl.BlockSpec((tm, D_MODEL), row), half, half, half,
                  _const_spec((GM_GROUPS, chunk, chunk)), _const_spec((GM_GROUPS, chunk, GM_CH)),
                  _const_spec((D_MODEL, D_MODEL))],
        out_specs=pl.BlockSpec((tm, D_MODEL), row),
        out_shape=jax.ShapeDtypeStruct((t, D_MODEL), F32),
        scratch_shapes=[pltpu.VMEM((tm, D_MODEL), BF16)],
        compiler_params=_params(("parallel",)),
        name=name,
    )(x, o, u, gv, ws, bs, wout)


def _mem_kv_body(mem_ref, g_ref, wk_ref, wv_ref, kg_ref, k_in, v_in, k_ref, v_ref, kb_ref, vb_ref, *, nb):
    del k_in, v_in
    m = _rms(mem_ref[...], g_ref[...]).astype(BF16)
    k = jnp.dot(m, wk_ref[...], preferred_element_type=F32)
    v = jnp.dot(m, wv_ref[...], preferred_element_type=F32)
    vb_ref[...] = v.astype(vb_ref.dtype)
    for h in range(X_HEADS):
        cols = slice(h * X_DH, (h + 1) * X_DH)
        kn = _rms(k[:, cols], kg_ref[...])
        kb_ref[:, cols] = kn.astype(kb_ref.dtype)
        for bi in range(nb):
            rows = slice(bi * MEM_LEN, (bi + 1) * MEM_LEN)
            k_ref[bi, :, h, :] = kn[rows]
            v_ref[bi, :, h, :] = v[rows, cols]


def _mem_kv(mem, g, wk, wv, kg, mk_all, mv_all, layer, tm, name):
    t = mem.shape[0]
    nb = tm // MEM_LEN
    blk = pl.BlockSpec((tm, D_MODEL), lambda i: (i, 0))
    out_spec = pl.BlockSpec((None, nb, MEM_LEN, X_HEADS, X_DH), lambda i: (layer, i, 0, 0, 0))
    return pl.pallas_call(
        functools.partial(_mem_kv_body, nb=nb),
        grid=(t // tm,),
        in_specs=[blk, _const_spec((1, D_MODEL)), _const_spec((D_MODEL, D_MODEL)),
                  _const_spec((D_MODEL, D_MODEL)), _const_spec((1, X_DH)), _ANY, _ANY],
        out_specs=[out_spec, out_spec, blk, blk],
        out_shape=[jax.ShapeDtypeStruct(mk_all.shape, F32), jax.ShapeDtypeStruct(mv_all.shape, F32),
                   jax.ShapeDtypeStruct((t, D_MODEL), BF16), jax.ShapeDtypeStruct((t, D_MODEL), BF16)],
        input_output_aliases={5: 0, 6: 1},
        compiler_params=_params(("parallel",)),
        name=name,
    )(mem, g, wk, wv, kg, mk_all, mv_all)


def _xattn_body(x_ref, g_ref, wq_ref, qg_ref, mk_ref, mv_ref, wo_ref, xo_ref, att_ref):
    tm = x_ref.shape[0]
    sub = min(SUB_ROWS, tm)
    flat = len(mk_ref.shape) == 2

    def project(j):
        rows = slice(j * sub, (j + 1) * sub)
        h = _rms(x_ref[rows, :], g_ref[...]).astype(BF16)
        return jnp.dot(h, wq_ref[...], preferred_element_type=F32)

    def attend(j, q):
        rows = slice(j * sub, (j + 1) * sub)
        for hh in range(X_HEADS):
            cols = slice(hh * X_DH, (hh + 1) * X_DH)
            mk = mk_ref[:, cols] if flat else mk_ref[:, hh, :]
            mv = mv_ref[:, cols] if flat else mv_ref[:, hh, :]
            qn = (_rms(q[:, cols], qg_ref[...]) * (X_DH ** -0.5)).astype(BF16)
            s = lax.dot_general(qn, mk.astype(BF16), _NT, preferred_element_type=F32)
            p = jnp.exp(s - jnp.max(s, axis=-1, keepdims=True))
            p = p * (1.0 / jnp.sum(p, axis=-1, keepdims=True))
            att_ref[rows, cols] = jnp.dot(p.astype(BF16), mv.astype(BF16),
                                          preferred_element_type=F32).astype(BF16)
        xo_ref[rows, :] = x_ref[rows, :] + jnp.dot(att_ref[rows, :], wo_ref[...],
                                                   preferred_element_type=F32)

    n_sub = tm // sub
    nxt = project(0)
    for j in range(n_sub):
        cur = nxt
        if j + 1 < n_sub:
            nxt = project(j + 1)
        attend(j, cur)


def _xattn(x, g, wq, qg, mk, mv, mem_spec, wo, tm, name):
    t = x.shape[0]
    blk = pl.BlockSpec((tm, D_MODEL), lambda i: (i, 0))
    return pl.pallas_call(
        _xattn_body,
        grid=(t // tm,),
        in_specs=[blk, _const_spec((1, D_MODEL)), _const_spec((D_MODEL, D_MODEL)),
                  _const_spec((1, X_DH)), mem_spec, mem_spec, _const_spec((D_MODEL, D_MODEL))],
        out_specs=blk,
        out_shape=jax.ShapeDtypeStruct((t, D_MODEL), F32),
        scratch_shapes=[pltpu.VMEM((tm, D_MODEL), BF16)],
        compiler_params=_params(("parallel",)),
        name=name,
    )(x, g, wq, qg, mk, mv, wo)


def _conv_ffn_body(x_ref, g_ref, wup_ref, cw_ref, cb_ref, wdn_ref, hist_ref, xo_ref, cst_ref,
                   gp_ref, carry_ref, act_ref, *, nb, rows_per_b, tiles_per_b):
    i = pl.program_id(0)
    r = rows_per_b
    x = x_ref[...]
    h = _rms(x, g_ref[...]).astype(BF16)
    first = (i % tiles_per_b) == 0
    for c in range(D_FF // FF_CHUNK):
        cs = slice(c * FF_CHUNK, (c + 1) * FF_CHUNK)
        g = jnp.dot(h, wup_ref[:, cs], preferred_element_type=F32)
        up = jnp.dot(h, wup_ref[:, D_FF + c * FF_CHUNK:D_FF + (c + 1) * FF_CHUNK],
                     preferred_element_type=F32)
        for bi in range(nb):
            rows = slice(bi * r, (bi + 1) * r)
            if tiles_per_b == 1:
                prev = hist_ref[bi, :, cs]
            else:
                prev = jnp.where(first, hist_ref[bi, :, cs], carry_ref[:, cs])
            gp_ref[6:8, :] = prev
            gp_ref[8:8 + r, :] = g[rows]
            conv = (cb_ref[:, cs] + cw_ref[0:1, cs] * gp_ref[6:6 + r, :]
                    + cw_ref[1:2, cs] * gp_ref[7:7 + r, :] + cw_ref[2:3, cs] * g[rows])
            act = conv * (1.0 / (1.0 + jnp.exp(-conv))) * up[rows]
            act_ref[rows, cs] = act.astype(BF16)
            last2 = gp_ref[6 + r:8 + r, :]
            cst_ref[bi, :, cs] = last2
            if tiles_per_b > 1:
                carry_ref[:, cs] = last2
    xo_ref[...] = x + jnp.dot(act_ref[...], wdn_ref[...], preferred_element_type=F32)


def _conv_ffn(x, g, wup, cw, cb, wdn, hist, tm, nb, tiles_per_b, name):
    t = x.shape[0]
    row = lambda i: (i, 0)
    blk = pl.BlockSpec((tm, D_MODEL), row)
    hist_spec = pl.BlockSpec((nb, CONV_W - 1, D_FF), lambda i: (i // tiles_per_b, 0, 0))
    return pl.pallas_call(
        functools.partial(_conv_ffn_body, nb=nb, rows_per_b=tm // nb, tiles_per_b=tiles_per_b),
        grid=(t // tm,),
        in_specs=[blk, _const_spec((1, D_MODEL)), _const_spec((D_MODEL, 2 * D_FF)),
                  _const_spec((CONV_W, D_FF)), _const_spec((1, D_FF)), _const_spec((D_FF, D_MODEL)),
                  hist_spec],
        out_specs=[blk, hist_spec],
        out_shape=[jax.ShapeDtypeStruct((t, D_MODEL), F32),
                   jax.ShapeDtypeStruct(hist.shape, F32)],
        scratch_shapes=[pltpu.VMEM((tm // nb + 8, FF_CHUNK), F32),
                        pltpu.VMEM((CONV_W - 1, D_FF), F32),
                        pltpu.VMEM((tm, D_FF), BF16)],
        compiler_params=_params(("arbitrary",)),
        name=name,
    )(x, g, wup, cw, cb, wdn, hist)


def _rope_angles(pos):
    inv = ROPE_THETA ** (-jnp.arange(ROT_HALF, dtype=F32) / ROT_HALF)
    ang = pos.astype(F32)[:, None] * inv[None, :]
    return jnp.cos(ang), jnp.sin(ang)


def _rope_tables(pos):
    cos, sin = _rope_angles(pos)
    n = pos.shape[0]
    ones = jnp.ones((n, DA_DK - ROT_DIM), F32)
    zeros_h = jnp.zeros((n, ROT_HALF), F32)
    zeros_r = jnp.zeros((n, DA_DK - ROT_DIM), F32)
    cos64 = jnp.concatenate([cos, cos, ones], axis=1)
    sa64 = jnp.concatenate([-sin, zeros_h, zeros_r], axis=1)
    sb64 = jnp.concatenate([zeros_h, sin, zeros_r], axis=1)
    two = lambda a: jnp.concatenate([a, a], axis=1)
    return two(cos64), two(sa64), two(sb64)


def _post_attention(x, o, u, gv, w, mk, mv, mem_spec, hist, seq, tm, tm_wide, tag):
    t = x.shape[0]
    row2 = lambda a: a.reshape(1, -1)
    chunk = min(seq, GM_CHUNK)
    ws = w['gm_w_s'][:, :chunk, :chunk]
    bs = jnp.broadcast_to(w['gm_b'][:, :chunk, None], (GM_GROUPS, chunk, GM_CH))
    x = _mix_out(x, o.reshape(t, DA_WIDTH), u, gv, ws, bs, w['w_out'], tm, chunk, f"mix_out_{tag}")
    xa_tm = min(tm_wide, seq)
    x = _xattn(x, row2(w['norm_x_g']), w['wq_c'], row2(w['xq_norm_g']), mk, mv, mem_spec,
               w['wo_c'], xa_tm, f"xattn_{tag}")
    nb = max(1, tm // seq)
    tiles_per_b = max(1, seq // tm)
    return _conv_ffn(x, row2(w['norm_ffn_g']), w['w_up'], w['conv_w'], row2(w['conv_b']),
                     w['w_down'], hist, tm, nb, tiles_per_b, f"conv_ffn_{tag}")


def kernel(x_prompt, x_sample, cache_da_k, cache_da_v, cache_mem_k, cache_mem_v, state_ffn_conv, mem_prompt, norm_mix_g, w_in, da_q_norm_g, da_k_norm_g, lambda_q1, lambda_k1, lambda_q2, lambda_k2, da_subln_g, gm_norm_g, gm_w_s, gm_b, w_out, norm_x_g, norm_mem_g, wq_c, wk_c, wv_c, wo_c, xq_norm_g, xk_norm_g, norm_ffn_g, w_up, conv_w, conv_b, w_down):
    bp, sp, _ = x_prompt.shape
    bs_, ss, _ = x_sample.shape
    depth = w_in.shape[0]
    past = cache_da_k.shape[2]
    tm_p = min(TOKEN_TILE, sp)
    tm_w = min(WIDE_TILE, sp)
    tm_s = bs_ * ss
    tm_mem = min(TOKEN_TILE, bp * MEM_LEN)
    assert sp % tm_p == 0 and sp % Q_TILE == 0 and sp % GM_CHUNK == 0
    assert sp % tm_w == 0 and tm_w % SUB_ROWS == 0
    assert ss <= GM_CHUNK and tm_s % SUBLANES == 0 and ss >= CONV_W - 1 and ss % SUBLANES == 0
    assert (bp * MEM_LEN) % tm_mem == 0 and tm_mem % MEM_LEN == 0

    pos_p = jnp.arange(sp)
    rope_p = _rope_tables(pos_p)
    cos_p, sin_p = _rope_angles(pos_p)
    rope_kp = (cos_p.T, sin_p.T)
    rope_s = tuple(jnp.tile(a, (bs_, 1)) for a in _rope_tables(past + jnp.arange(ss)))
    cache_kt = jnp.transpose(cache_da_k, (0, 1, 3, 4, 5, 2)).reshape(depth, bs_, DA_QK_COLS, past)
    hist_p = jnp.zeros((bp, CONV_W - 1, D_FF), F32)
    mem_flat = mem_prompt.reshape(bp * MEM_LEN, D_MODEL)

    kt_all = lax.empty((depth, bp, DA_QK_COLS, sp), F32)
    v_all = lax.empty((depth, bp, sp, DA_HEADS, DA_DV), F32)
    mk_all = lax.empty((depth, bp, MEM_LEN, X_HEADS, X_DH), F32)
    mv_all = lax.empty((depth, bp, MEM_LEN, X_HEADS, X_DH), F32)

    xp = x_prompt.reshape(bp * sp, D_MODEL)
    xs = x_sample.reshape(tm_s, D_MODEL)
    outs = {n: [] for n in ('fc_p', 'dk_s', 'dv_s', 'gv_s', 'fc_s')}
    row2 = lambda a: a.reshape(1, -1)
    for l in range(depth):
        lam_init = 0.8 - 0.6 * math.exp(-0.3 * l)
        w = dict(gm_w_s=gm_w_s[l], gm_b=gm_b[l], w_out=w_out[l].astype(BF16),
                 norm_x_g=norm_x_g[l], wq_c=wq_c[l].astype(BF16), wo_c=wo_c[l].astype(BF16),
                 xq_norm_g=xq_norm_g[l], norm_ffn_g=norm_ffn_g[l], w_up=w_up[l].astype(BF16),
                 conv_w=conv_w[l], conv_b=conv_b[l], w_down=w_down[l].astype(BF16))
        w_in_l = w_in[l].astype(BF16)
        w_main = jnp.concatenate([w_in_l[:, :DA_QK_COLS], w_in_l[:, 2 * DA_QK_COLS:]], axis=1)
        w_kt = w_in_l[:, DA_QK_COLS:2 * DA_QK_COLS].T
        qg = jnp.tile(da_q_norm_g[l], 2).reshape(1, -1)
        kg = jnp.tile(da_k_norm_g[l], 2).reshape(1, -1)
        kg_col = jnp.broadcast_to(da_k_norm_g[l][:, None], (DA_DK, SUB_ROWS))
        gmg = jnp.tile(gm_norm_g[l], GM_GROUPS).reshape(1, -1)
        lams = [row2(a[l]) for a in (lambda_q1, lambda_k1, lambda_q2, lambda_k2)]
        subg = row2(da_subln_g[l])
        g_mix = row2(norm_mix_g[l])

        mk_all, mv_all, mkb, mvb = _mem_kv(mem_flat, row2(norm_mem_g[l]), wk_c[l].astype(BF16),
                                           wv_c[l].astype(BF16), row2(xk_norm_g[l]), mk_all, mv_all, l,
                                           tm_mem, f"mem_kv_{l}")
        q, kt_all, v_all, vb, u, gv = _mix_in_prompt(xp, g_mix, w_main, w_kt, qg, kg_col, gmg, rope_p, rope_kp,
                                                     kt_all, v_all, l, sp, tm_w, f"mix_in_p{l}")
        o = _da_prompt(q.reshape(bp, sp, DA_QK_COLS), kt_all, vb.reshape(bp, sp, DA_WIDTH), l, lams, subg,
                       lam_init, f"da_p{l}")
        mem_spec_p = pl.BlockSpec((MEM_LEN, D_MODEL), lambda i: (i // (sp // tm_w), 0))
        xp, cst = _post_attention(xp, o, u, gv, w, mkb, mvb, mem_spec_p, hist_p, sp, tm_p, tm_w, f"p{l}")
        outs['fc_p'].append(cst)

        q, k, v, u, gv = _mix_in_sample(xs, g_mix, w_in_l, qg, kg, gmg, rope_s, f"mix_in_s{l}")
        o = _da_sample(q.reshape(bs_, ss, DA_QK_COLS), k.reshape(bs_, ss, DA_QK_COLS),
                       v.reshape(bs_, ss, DA_WIDTH), cache_kt, cache_da_v, l, lams, subg, lam_init,
                       f"da_s{l}")
        mem_spec_s = pl.BlockSpec((None, None, MEM_LEN, X_HEADS, X_DH), lambda i, l=l: (l, i, 0, 0, 0))
        xs, cst = _post_attention(xs, o, u, gv, w, cache_mem_k, cache_mem_v, mem_spec_s, state_ffn_conv[l],
                                  ss, tm_s, tm_s, f"s{l}")
        outs['dk_s'].append(k.reshape(bs_, ss, DA_HEADS, 2, DA_DK))
        outs['dv_s'].append(v.reshape(bs_, ss, DA_HEADS, DA_DV))
        outs['gv_s'].append(gv.reshape(bs_, ss, GM_GROUPS, GM_CH))
        outs['fc_s'].append(cst)

    st = {n: jnp.stack(a) for n, a in outs.items()}
    dk_p = jnp.transpose(kt_all.reshape(depth, bp, DA_HEADS, 2, DA_DK, sp), (0, 1, 5, 2, 3, 4))
    return (xp.reshape(bp, sp, D_MODEL), xs.reshape(bs_, ss, D_MODEL), dk_p, v_all, mk_all, mv_all,
            st['fc_p'], st['dk_s'], st['dv_s'], st['gv_s'], st['fc_s'])
```

```python
import functools
import math

import numpy as np
import jax
import jax.numpy as jnp
from jax import lax
from jax.experimental import pallas as pl
from jax.experimental.pallas import tpu as pltpu

F32 = jnp.float32
BF16 = jnp.bfloat16

D_MODEL = 1024
CHUNK = 64
DA_HEADS = 4
DA_DK = 64
DA_DV = 128
DA_QK_COLS = DA_HEADS * 2 * DA_DK
DA_WIDTH = DA_HEADS * DA_DV
ROT_DIM = 16
ROT_HALF = ROT_DIM // 2
ROPE_THETA = 500000.0
GM_GROUPS = 4
GM_CH = 128
GM_WIDTH = GM_GROUPS * GM_CH
GM_CHUNK = 128
MEM_LEN = 256
X_HEADS = 4
X_DH = 256
D_FF = 2816
CONV_W = 3
EPS = 1e-6
IN_COLS = 2 * DA_QK_COLS + DA_WIDTH + 2 * GM_WIDTH
MAIN_COLS = IN_COLS - DA_QK_COLS
NEG = float(np.finfo(np.float32).min)
Q_SCALE = (DA_DK ** -0.5) * math.log2(math.e)

LANES = 128
SUBLANES = 8
VMEM_LIMIT_BYTES = 56 * 1024 * 1024
FF_CHUNK = 256
TOKEN_TILE = 512
WIDE_TILE = 1024
Q_TILE = 256
SUB_ROWS = 512

_NT = (((1,), (1,)), ((), ()))


def _params(sem):
    return pltpu.CompilerParams(dimension_semantics=sem, vmem_limit_bytes=VMEM_LIMIT_BYTES)


def _const_spec(shape):
    n = len(shape)
    return pl.BlockSpec(shape, lambda *_: (0,) * n)


_ANY = pl.BlockSpec(memory_space=pl.ANY)


def _rms(x, g):
    return (x * lax.rsqrt(jnp.mean(x * x, axis=-1, keepdims=True) + EPS)) * g


def _gelu(x):
    return x * (0.5 * (1.0 + jnp.tanh(0.7978845608028654 * (x + 0.044715 * (x * x * x)))))


def _qk_slab(t, gain, cosv, sa, sb):
    lo = lax.broadcasted_iota(jnp.int32, (1, LANES), 1) < DA_DK
    sq = t * t
    s_lo = jnp.sum(jnp.where(lo, sq, 0.0), axis=-1, keepdims=True)
    s_hi = jnp.sum(jnp.where(lo, 0.0, sq), axis=-1, keepdims=True)
    ms = jnp.where(lo, s_lo, s_hi) * (1.0 / DA_DK)
    y = (t * lax.rsqrt(ms + EPS)) * gain
    return y * cosv + pltpu.roll(y, LANES - ROT_HALF, 1) * sa + pltpu.roll(y, ROT_HALF, 1) * sb


def _gmlp_cols(z, c0, gmg_ref, u_ref, gv_ref, rows=slice(None)):
    u_ref[rows, :] = _gelu(z[:, c0:c0 + GM_WIDTH]).astype(u_ref.dtype)
    c1 = c0 + GM_WIDTH
    for s in range(GM_GROUPS):
        cols = slice(s * GM_CH, (s + 1) * GM_CH)
        t = _gelu(z[:, c1 + s * GM_CH:c1 + (s + 1) * GM_CH])
        gv_ref[rows, cols] = _rms(t, gmg_ref[:, cols]).astype(gv_ref.dtype)


def _mix_in_sample_body(x_ref, g_ref, w_ref, qg_ref, kg_ref, gmg_ref, cos_ref, sa_ref, sb_ref,
                        q_ref, k_ref, v_ref, u_ref, gv_ref):
    h = _rms(x_ref[...], g_ref[...]).astype(BF16)
    z = jnp.dot(h, w_ref[...], preferred_element_type=F32)
    cosv, sa, sb = cos_ref[...], sa_ref[...], sb_ref[...]
    for s in range(DA_HEADS):
        cols = slice(s * LANES, (s + 1) * LANES)
        q_ref[:, cols] = (_qk_slab(z[:, cols], qg_ref[...], cosv, sa, sb) * Q_SCALE).astype(q_ref.dtype)
        kc = slice(DA_QK_COLS + s * LANES, DA_QK_COLS + (s + 1) * LANES)
        k_ref[:, cols] = _qk_slab(z[:, kc], kg_ref[...], cosv, sa, sb)
    c2 = 2 * DA_QK_COLS
    v_ref[...] = z[:, c2:c2 + DA_WIDTH]
    _gmlp_cols(z, c2 + DA_WIDTH, gmg_ref, u_ref, gv_ref)


def _mix_in_sample(x, g, w, qg, kg, gmg, rope, name):
    t = x.shape[0]
    row = pl.BlockSpec((t, DA_QK_COLS), lambda i: (0, 0))
    rope_spec = _const_spec((t, LANES))
    return pl.pallas_call(
        _mix_in_sample_body,
        grid=(1,),
        in_specs=[_const_spec((t, D_MODEL)), _const_spec((1, D_MODEL)), _const_spec((D_MODEL, IN_COLS)),
                  _const_spec((1, LANES)), _const_spec((1, LANES)), _const_spec((1, GM_WIDTH)),
                  rope_spec, rope_spec, rope_spec],
        out_specs=[row] * 5,
        out_shape=[jax.ShapeDtypeStruct((t, DA_QK_COLS), BF16),
                   jax.ShapeDtypeStruct((t, DA_QK_COLS), F32),
                   jax.ShapeDtypeStruct((t, DA_WIDTH), F32),
                   jax.ShapeDtypeStruct((t, GM_WIDTH), BF16),
                   jax.ShapeDtypeStruct((t, GM_WIDTH), F32)],
        compiler_params=_params(("arbitrary",)),
        name=name,
    )(x, g, w, qg, kg, gmg, *rope)


def _mix_in_prompt_body(x_ref, g_ref, w_ref, wk_ref, qg_ref, kg_ref, gmg_ref, cos_ref, sa_ref, sb_ref,
                        cosk_ref, sink_ref, kt_in, v_in, q_ref, kt_ref, v_ref, vb_ref, u_ref, gv_ref):
    del kt_in, v_in
    tm = x_ref.shape[0]

    def project(j):
        rows = slice(j * SUB_ROWS, (j + 1) * SUB_ROWS)
        h = _rms(x_ref[rows, :], g_ref[...]).astype(BF16)
        z = jnp.dot(h, w_ref[...], preferred_element_type=F32)
        zk = lax.dot_general(wk_ref[...], h, _NT, preferred_element_type=F32)
        return z, zk

    def finish(j, z, zk):
        rows = slice(j * SUB_ROWS, (j + 1) * SUB_ROWS)
        cosv, sa, sb = cos_ref[rows, :], sa_ref[rows, :], sb_ref[rows, :]
        for s in range(DA_HEADS):
            cols = slice(s * LANES, (s + 1) * LANES)
            q_ref[rows, cols] = (_qk_slab(z[:, cols], qg_ref[...], cosv, sa, sb) * Q_SCALE).astype(q_ref.dtype)
            v_ref[rows, s, :] = z[:, DA_QK_COLS + s * DA_DV:DA_QK_COLS + (s + 1) * DA_DV]
        vb_ref[rows, :] = z[:, DA_QK_COLS:DA_QK_COLS + DA_WIDTH].astype(vb_ref.dtype)
        ck, sk, kg = cosk_ref[:, rows], sink_ref[:, rows], kg_ref[...]
        for grp in range(2 * DA_HEADS):
            r0 = grp * DA_DK
            t = zk[r0:r0 + DA_DK, :]
            inv = lax.rsqrt(jnp.sum(t * t, axis=0, keepdims=True) * (1.0 / DA_DK) + EPS)
            y = (t * inv) * kg
            ya, yb = y[:ROT_HALF], y[ROT_HALF:ROT_DIM]
            kt_ref[r0:r0 + ROT_HALF, rows] = ya * ck - yb * sk
            kt_ref[r0 + ROT_HALF:r0 + ROT_DIM, rows] = yb * ck + ya * sk
            kt_ref[r0 + ROT_DIM:r0 + DA_DK, rows] = y[ROT_DIM:]
        _gmlp_cols(z, DA_QK_COLS + DA_WIDTH, gmg_ref, u_ref, gv_ref, rows)

    n_sub = tm // SUB_ROWS
    nxt = project(0)
    for j in range(n_sub):
        cur = nxt
        if j + 1 < n_sub:
            nxt = project(j + 1)
        finish(j, *cur)


def _mix_in_prompt(x, g, w, wk, qg, kg_col, gmg, rope, rope_k, kt_all, v_all, layer, seq, tm, name):
    t = x.shape[0]
    per_b = seq // tm
    row = lambda i: (i, 0)
    rope_spec = pl.BlockSpec((tm, LANES), lambda i: (i % per_b, 0))
    ropek_spec = pl.BlockSpec((SUBLANES, tm), lambda i: (0, i % per_b))
    half = pl.BlockSpec((tm, DA_QK_COLS), row)
    kt_spec = pl.BlockSpec((None, None, DA_QK_COLS, tm), lambda i: (layer, i // per_b, 0, i % per_b))
    v_spec = pl.BlockSpec((None, None, tm, DA_HEADS, DA_DV), lambda i: (layer, i // per_b, i % per_b, 0, 0))
    return pl.pallas_call(
        _mix_in_prompt_body,
        grid=(t // tm,),
        in_specs=[pl.BlockSpec((tm, D_MODEL), row), _const_spec((1, D_MODEL)),
                  _const_spec((D_MODEL, MAIN_COLS)), _const_spec((DA_QK_COLS, D_MODEL)),
                  _const_spec((1, LANES)), _const_spec((DA_DK, SUB_ROWS)), _const_spec((1, GM_WIDTH)),
                  rope_spec, rope_spec, rope_spec, ropek_spec, ropek_spec, _ANY, _ANY],
        out_specs=[half, kt_spec, v_spec, half, half, half],
        out_shape=[jax.ShapeDtypeStruct((t, DA_QK_COLS), BF16),
                   jax.ShapeDtypeStruct(kt_all.shape, F32),
                   jax.ShapeDtypeStruct(v_all.shape, F32),
                   jax.ShapeDtypeStruct((t, DA_WIDTH), BF16),
                   jax.ShapeDtypeStruct((t, GM_WIDTH), BF16),
                   jax.ShapeDtypeStruct((t, GM_WIDTH), BF16)],
        input_output_aliases={12: 1, 13: 2},
        compiler_params=_params(("parallel",)),
        name=name,
    )(x, g, w, wk, qg, kg_col, gmg, *rope, *rope_k, kt_all, v_all)


def _lambda(lq1, lk1, lq2, lk2, lam_init):
    a = jnp.sum(lq1[...] * lk1[...], axis=-1, keepdims=True)
    b = jnp.sum(lq2[...] * lk2[...], axis=-1, keepdims=True)
    return jnp.exp(a) - jnp.exp(b) + lam_init


def _subln(o, g, lam_init):
    return _rms(o, g) * (1.0 - lam_init)


def _split_maps(q):
    lo = lax.broadcasted_iota(jnp.int32, (1, LANES), 1) < DA_DK
    zero = jnp.zeros_like(q)
    return jnp.where(lo, q, zero), jnp.where(lo, zero, q)


def _da_prompt_body(lq1, lk1, lq2, lk2, subg_ref, q_ref, kt_ref, v_ref, o_ref, *, lam_init, bq):
    seq = q_ref.shape[1]
    lam = _lambda(lq1, lk1, lq2, lk2, lam_init)
    qm = _split_maps(q_ref[0])
    kt = kt_ref[...].astype(BF16)
    va = v_ref[0]
    mask = ((lax.broadcasted_iota(jnp.int32, (bq, bq), 1) // CHUNK)
            <= (lax.broadcasted_iota(jnp.int32, (bq, bq), 0) // CHUNK))
    def scores(qi):
        lo, hi = qi * bq, (qi + 1) * bq
        out = []
        for m in range(2):
            q = qm[m][lo:hi]
            sd = jnp.dot(q, kt[:, lo:hi], preferred_element_type=F32)
            so = jnp.dot(q, kt[:, :lo], preferred_element_type=F32) if qi > 0 else None
            out.append((sd, so))
        return out

    nq = seq // bq
    nxt = scores(0)
    for qi in range(nq):
        lo, hi = qi * bq, (qi + 1) * bq
        cur = nxt
        if qi + 1 < nq:
            nxt = scores(qi + 1)
        probs = []
        for m in range(2):
            sd, so = cur[m]
            sd = jnp.where(mask, sd, NEG)
            mx = jnp.max(sd, axis=-1, keepdims=True)
            if qi > 0:
                mx = jnp.maximum(mx, jnp.max(so, axis=-1, keepdims=True))
            pd = jnp.exp2(sd - mx)
            den = jnp.sum(pd, axis=-1, keepdims=True)
            po = None
            if qi > 0:
                po = jnp.exp2(so - mx)
                den = den + jnp.sum(po, axis=-1, keepdims=True)
            probs.append((pd, po, den))
        c1 = 1.0 / probs[0][2]
        r = lam * probs[0][2] / probs[1][2]
        o = jnp.dot((probs[0][0] - r * probs[1][0]).astype(BF16), va[lo:hi], preferred_element_type=F32)
        if qi > 0:
            o = o + jnp.dot((probs[0][1] - r * probs[1][1]).astype(BF16), va[:lo],
                            preferred_element_type=F32)
        o_ref[0, lo:hi, :] = _subln(o * c1, subg_ref[...], lam_init).astype(o_ref.dtype)


def _da_prompt(q, kt_all, vb, layer, lams, subg, lam_init, name):
    b, s, _ = q.shape
    lam_spec = _const_spec((1, DA_DK))
    q_spec = pl.BlockSpec((1, s, LANES), lambda bi, h: (bi, 0, h))
    kt_spec = pl.BlockSpec((None, None, 2 * DA_DK, s), lambda bi, h: (layer, bi, h, 0))
    return pl.pallas_call(
        functools.partial(_da_prompt_body, lam_init=lam_init, bq=Q_TILE),
        grid=(b, DA_HEADS),
        in_specs=[lam_spec, lam_spec, lam_spec, lam_spec, _const_spec((1, DA_DV)),
                  q_spec, kt_spec, q_spec],
        out_specs=q_spec,
        out_shape=jax.ShapeDtypeStruct((b, s, DA_WIDTH), BF16),
        compiler_params=_params(("parallel", "parallel")),
        name=name,
    )(*lams, subg, q, kt_all, vb)


def _da_sample_body(lq1, lk1, lq2, lk2, subg_ref, q_ref, ktp_ref, kn_ref, vp_ref, vn_ref, o_ref,
                    *, lam_init, past, sq):
    lam = _lambda(lq1, lk1, lq2, lk2, lam_init)
    q_chunk = (past + lax.broadcasted_iota(jnp.int32, (sq, 1), 0)) // CHUNK
    mask_p = (lax.broadcasted_iota(jnp.int32, (1, past), 1) // CHUNK) <= q_chunk
    mask_n = ((past + lax.broadcasted_iota(jnp.int32, (1, sq), 1)) // CHUNK) <= q_chunk
    for h in range(DA_HEADS):
        cols = slice(h * LANES, (h + 1) * LANES)
        qm = _split_maps(q_ref[0, :, cols])
        ktp = ktp_ref[cols, :].astype(BF16)
        kn = kn_ref[0, :, cols].astype(BF16)
        probs = []
        for m in range(2):
            sp = jnp.where(mask_p, jnp.dot(qm[m], ktp, preferred_element_type=F32), NEG)
            sn = jnp.where(mask_n, lax.dot_general(qm[m], kn, _NT, preferred_element_type=F32), NEG)
            mx = jnp.maximum(jnp.max(sp, axis=-1, keepdims=True), jnp.max(sn, axis=-1, keepdims=True))
            pp = jnp.exp2(sp - mx)
            pn = jnp.exp2(sn - mx)
            inv = 1.0 / (jnp.sum(pp, axis=-1, keepdims=True) + jnp.sum(pn, axis=-1, keepdims=True))
            probs.append((pp * inv, pn * inv))
        ap = (probs[0][0] - lam * probs[1][0]).astype(BF16)
        an = (probs[0][1] - lam * probs[1][1]).astype(BF16)
        o = (jnp.dot(ap, vp_ref[:, h, :].astype(BF16), preferred_element_type=F32)
             + jnp.dot(an, vn_ref[0, :, cols].astype(BF16), preferred_element_type=F32))
        o_ref[0, :, cols] = _subln(o, subg_ref[...], lam_init).astype(o_ref.dtype)


def _da_sample(q, k_new, v_new, cache_kt, cache_v, layer, lams, subg, lam_init, name):
    b, sq, _ = q.shape
    past = cache_kt.shape[3]
    lam_spec = _const_spec((1, DA_DK))
    new_spec = pl.BlockSpec((1, sq, DA_QK_COLS), lambda bi: (bi, 0, 0))
    ktp_spec = pl.BlockSpec((None, None, DA_QK_COLS, past), lambda bi: (layer, bi, 0, 0))
    vp_spec = pl.BlockSpec((None, None, past, DA_HEADS, DA_DV), lambda bi: (layer, bi, 0, 0, 0))
    return pl.pallas_call(
        functools.partial(_da_sample_body, lam_init=lam_init, past=past, sq=sq),
        grid=(b,),
        in_specs=[lam_spec, lam_spec, lam_spec, lam_spec, _const_spec((1, DA_DV)),
                  new_spec, ktp_spec, new_spec, vp_spec, new_spec],
        out_specs=new_spec,
        out_shape=jax.ShapeDtypeStruct((b, sq, DA_WIDTH), BF16),
        compiler_params=_params(("parallel",)),
        name=name,
    )(*lams, subg, q, cache_kt, k_new, cache_v, v_new)


def _mem_kv_body(mem_ref, g_ref, wk_ref, wv_ref, kg_ref, k_in, v_in, k_ref, v_ref, kb_ref, vb_ref, *, nb):
    del k_in, v_in
    m = _rms(mem_ref[...], g_ref[...]).astype(BF16)
    k = jnp.dot(m, wk_ref[...], preferred_element_type=F32)
    v = jnp.dot(m, wv_ref[...], preferred_element_type=F32)
    vb_ref[...] = v.astype(vb_ref.dtype)
    for h in range(X_HEADS):
        cols = slice(h * X_DH, (h + 1) * X_DH)
        kn = _rms(k[:, cols], kg_ref[...])
        kb_ref[:, cols] = kn.astype(kb_ref.dtype)
        for bi in range(nb):
            rows = slice(bi * MEM_LEN, (bi + 1) * MEM_LEN)
            k_ref[bi, :, h, :] = kn[rows]
            v_ref[bi, :, h, :] = v[rows, cols]


def _mem_kv(mem, g, wk, wv, kg, mk_all, mv_all, layer, tm, name):
    t = mem.shape[0]
    nb = tm // MEM_LEN
    blk = pl.BlockSpec((tm, D_MODEL), lambda i: (i, 0))
    out_spec = pl.BlockSpec((None, nb, MEM_LEN, X_HEADS, X_DH), lambda i: (layer, i, 0, 0, 0))
    return pl.pallas_call(
        functools.partial(_mem_kv_body, nb=nb),
        grid=(t // tm,),
        in_specs=[blk, _const_spec((1, D_MODEL)), _const_spec((D_MODEL, D_MODEL)),
                  _const_spec((D_MODEL, D_MODEL)), _const_spec((1, X_DH)), _ANY, _ANY],
        out_specs=[out_spec, out_spec, blk, blk],
        out_shape=[jax.ShapeDtypeStruct(mk_all.shape, F32), jax.ShapeDtypeStruct(mv_all.shape, F32),
                   jax.ShapeDtypeStruct((t, D_MODEL), BF16), jax.ShapeDtypeStruct((t, D_MODEL), BF16)],
        input_output_aliases={5: 0, 6: 1},
        compiler_params=_params(("parallel",)),
        name=name,
    )(mem, g, wk, wv, kg, mk_all, mv_all)


def _mix_xattn_body(x_ref, o_ref, u_ref, gv_ref, ws_ref, bs_ref, wout_ref, g_ref, wq_ref, qg_ref,
                    mk_ref, mv_ref, wo_ref, xo_ref, cat_ref, att_ref, *, chunk):
    tm = x_ref.shape[0]
    sub = min(SUB_ROWS, tm)
    flat = len(mk_ref.shape) == 2
    tri = (lax.broadcasted_iota(jnp.int32, (chunk, chunk), 1)
           <= lax.broadcasted_iota(jnp.int32, (chunk, chunk), 0))
    ws = [jnp.where(tri, ws_ref[g], 0.0).astype(BF16) for g in range(GM_GROUPS)]

    def mix_out(j):
        r0 = j * sub
        rows = slice(r0, r0 + sub)
        cat_ref[rows, :DA_WIDTH] = o_ref[rows, :]
        for g in range(GM_GROUPS):
            cols = slice(g * GM_CH, (g + 1) * GM_CH)
            bias = bs_ref[g]
            for c in range(sub // chunk):
                cr = slice(r0 + c * chunk, r0 + (c + 1) * chunk)
                s = jnp.dot(ws[g], gv_ref[cr, cols].astype(BF16), preferred_element_type=F32) + bias
                cat_ref[cr, DA_WIDTH + g * GM_CH:DA_WIDTH + (g + 1) * GM_CH] = (
                    u_ref[cr, cols].astype(F32) * s).astype(BF16)
        x1 = x_ref[rows, :] + jnp.dot(cat_ref[rows, :], wout_ref[...], preferred_element_type=F32)
        h = _rms(x1, g_ref[...]).astype(BF16)
        return x1, jnp.dot(h, wq_ref[...], preferred_element_type=F32)

    def attend(j, x1, q):
        rows = slice(j * sub, (j + 1) * sub)
        for hh in range(X_HEADS):
            cols = slice(hh * X_DH, (hh + 1) * X_DH)
            mk = mk_ref[:, cols] if flat else mk_ref[:, hh, :]
            mv = mv_ref[:, cols] if flat else mv_ref[:, hh, :]
            qn = (_rms(q[:, cols], qg_ref[...]) * (X_DH ** -0.5)).astype(BF16)
            s = lax.dot_general(qn, mk.astype(BF16), _NT, preferred_element_type=F32)
            p = jnp.exp(s - jnp.max(s, axis=-1, keepdims=True))
            p = p * (1.0 / jnp.sum(p, axis=-1, keepdims=True))
            att_ref[rows, cols] = jnp.dot(p.astype(BF16), mv.astype(BF16),
                                          preferred_element_type=F32).astype(BF16)
        xo_ref[rows, :] = x1 + jnp.dot(att_ref[rows, :], wo_ref[...], preferred_element_type=F32)

    n_sub = tm // sub
    nxt = mix_out(0)
    for j in range(n_sub):
        cur = nxt
        if j + 1 < n_sub:
            nxt = mix_out(j + 1)
        attend(j, *cur)


def _mix_xattn(x, o, u, gv, ws, bs, wout, g, wq, qg, mk, mv, mem_spec, wo, tm, chunk, name):
    t = x.shape[0]
    row = lambda i: (i, 0)
    blk = pl.BlockSpec((tm, D_MODEL), row)
    half = pl.BlockSpec((tm, DA_WIDTH), row)
    sq = _const_spec((D_MODEL, D_MODEL))
    return pl.pallas_call(
        functools.partial(_mix_xattn_body, chunk=chunk),
        grid=(t // tm,),
        in_specs=[blk, half, half, half, _const_spec((GM_GROUPS, chunk, chunk)),
                  _const_spec((GM_GROUPS, chunk, GM_CH)), sq, _const_spec((1, D_MODEL)), sq,
                  _const_spec((1, X_DH)), mem_spec, mem_spec, sq],
        out_specs=blk,
        out_shape=jax.ShapeDtypeStruct((t, D_MODEL), F32),
        scratch_shapes=[pltpu.VMEM((tm, D_MODEL), BF16), pltpu.VMEM((tm, D_MODEL), BF16)],
        compiler_params=_params(("parallel",)),
        name=name,
    )(x, o, u, gv, ws, bs, wout, g, wq, qg, mk, mv, wo)


def _conv_ffn_body(x_ref, g_ref, wup_ref, cw_ref, cb_ref, wdn_ref, hist_ref, xo_ref, cst_ref,
                   gp_ref, carry_ref, act_ref, *, nb, rows_per_b, tiles_per_b):
    i = pl.program_id(0)
    r = rows_per_b
    x = x_ref[...]
    h = _rms(x, g_ref[...]).astype(BF16)
    first = (i % tiles_per_b) == 0
    def gate_up(c):
        g = jnp.dot(h, wup_ref[:, c * FF_CHUNK:(c + 1) * FF_CHUNK], preferred_element_type=F32)
        up = jnp.dot(h, wup_ref[:, D_FF + c * FF_CHUNK:D_FF + (c + 1) * FF_CHUNK],
                     preferred_element_type=F32)
        return g, up

    n_chunks = D_FF // FF_CHUNK
    nxt = gate_up(0)
    for c in range(n_chunks):
        cs = slice(c * FF_CHUNK, (c + 1) * FF_CHUNK)
        g, up = nxt
        if c + 1 < n_chunks:
            nxt = gate_up(c + 1)
        for bi in range(nb):
            rows = slice(bi * r, (bi + 1) * r)
            if tiles_per_b == 1:
                prev = hist_ref[bi, :, cs]
            else:
                prev = jnp.where(first, hist_ref[bi, :, cs], carry_ref[:, cs])
            gp_ref[6:8, :] = prev
            gp_ref[8:8 + r, :] = g[rows]
            conv = (cb_ref[:, cs] + cw_ref[0:1, cs] * gp_ref[6:6 + r, :]
                    + cw_ref[1:2, cs] * gp_ref[7:7 + r, :] + cw_ref[2:3, cs] * g[rows])
            act = conv * (1.0 / (1.0 + jnp.exp(-conv))) * up[rows]
            act_ref[rows, cs] = act.astype(BF16)
            last2 = gp_ref[6 + r:8 + r, :]
            cst_ref[bi, :, cs] = last2
            if tiles_per_b > 1:
                carry_ref[:, cs] = last2
    xo_ref[...] = x + jnp.dot(act_ref[...], wdn_ref[...], preferred_element_type=F32)


def _conv_ffn(x, g, wup, cw, cb, wdn, hist, tm, nb, tiles_per_b, name):
    t = x.shape[0]
    row = lambda i: (i, 0)
    blk = pl.BlockSpec((tm, D_MODEL), row)
    hist_spec = pl.BlockSpec((nb, CONV_W - 1, D_FF), lambda i: (i // tiles_per_b, 0, 0))
    return pl.pallas_call(
        functools.partial(_conv_ffn_body, nb=nb, rows_per_b=tm // nb, tiles_per_b=tiles_per_b),
        grid=(t // tm,),
        in_specs=[blk, _const_spec((1, D_MODEL)), _const_spec((D_MODEL, 2 * D_FF)),
                  _const_spec((CONV_W, D_FF)), _const_spec((1, D_FF)), _const_spec((D_FF, D_MODEL)),
                  hist_spec],
        out_specs=[blk, hist_spec],
        out_shape=[jax.ShapeDtypeStruct((t, D_MODEL), F32),
                   jax.ShapeDtypeStruct(hist.shape, F32)],
        scratch_shapes=[pltpu.VMEM((tm // nb + 8, FF_CHUNK), F32),
                        pltpu.VMEM((CONV_W - 1, D_FF), F32),
                        pltpu.VMEM((tm, D_FF), BF16)],
        compiler_params=_params(("arbitrary",)),
        name=name,
    )(x, g, wup, cw, cb, wdn, hist)


def _rope_angles(pos):
    inv = ROPE_THETA ** (-jnp.arange(ROT_HALF, dtype=F32) / ROT_HALF)
    ang = pos.astype(F32)[:, None] * inv[None, :]
    return jnp.cos(ang), jnp.sin(ang)


def _rope_tables(pos):
    cos, sin = _rope_angles(pos)
    n = pos.shape[0]
    ones = jnp.ones((n, DA_DK - ROT_DIM), F32)
    zeros_h = jnp.zeros((n, ROT_HALF), F32)
    zeros_r = jnp.zeros((n, DA_DK - ROT_DIM), F32)
    cos64 = jnp.concatenate([cos, cos, ones], axis=1)
    sa64 = jnp.concatenate([-sin, zeros_h, zeros_r], axis=1)
    sb64 = jnp.concatenate([zeros_h, sin, zeros_r], axis=1)
    two = lambda a: jnp.concatenate([a, a], axis=1)
    return two(cos64), two(sa64), two(sb64)


def _post_attention(x, o, u, gv, w, mk, mv, mem_spec, hist, seq, tm, tm_wide, tag):
    t = x.shape[0]
    row2 = lambda a: a.reshape(1, -1)
    chunk = min(seq, GM_CHUNK)
    ws = w['gm_w_s'][:, :chunk, :chunk]
    bs = jnp.broadcast_to(w['gm_b'][:, :chunk, None], (GM_GROUPS, chunk, GM_CH))
    x = _mix_xattn(x, o.reshape(t, DA_WIDTH), u, gv, ws, bs, w['w_out'], row2(w['norm_x_g']), w['wq_c'],
                   row2(w['xq_norm_g']), mk, mv, mem_spec, w['wo_c'], min(tm_wide, seq), chunk,
                   f"mix_xattn_{tag}")
    nb = max(1, tm // seq)
    tiles_per_b = max(1, seq // tm)
    return _conv_ffn(x, row2(w['norm_ffn_g']), w['w_up'], w['conv_w'], row2(w['conv_b']),
                     w['w_down'], hist, tm, nb, tiles_per_b, f"conv_ffn_{tag}")


def kernel(x_prompt, x_sample, cache_da_k, cache_da_v, cache_mem_k, cache_mem_v, state_ffn_conv, mem_prompt, norm_mix_g, w_in, da_q_norm_g, da_k_norm_g, lambda_q1, lambda_k1, lambda_q2, lambda_k2, da_subln_g, gm_norm_g, gm_w_s, gm_b, w_out, norm_x_g, norm_mem_g, wq_c, wk_c, wv_c, wo_c, xq_norm_g, xk_norm_g, norm_ffn_g, w_up, conv_w, conv_b, w_down):
    bp, sp, _ = x_prompt.shape
    bs_, ss, _ = x_sample.shape
    depth = w_in.shape[0]
    past = cache_da_k.shape[2]
    tm_p = min(TOKEN_TILE, sp)
    tm_w = min(WIDE_TILE, sp)
    tm_s = bs_ * ss
    tm_mem = min(TOKEN_TILE, bp * MEM_LEN)
    assert sp % tm_p == 0 and sp % Q_TILE == 0 and sp % GM_CHUNK == 0
    assert sp % tm_w == 0 and tm_w % SUB_ROWS == 0
    assert ss <= GM_CHUNK and tm_s % SUBLANES == 0 and ss >= CONV_W - 1 and ss % SUBLANES == 0
    assert (bp * MEM_LEN) % tm_mem == 0 and tm_mem % MEM_LEN == 0

    pos_p = jnp.arange(sp)
    rope_p = _rope_tables(pos_p)
    cos_p, sin_p = _rope_angles(pos_p)
    rope_kp = (cos_p.T, sin_p.T)
    rope_s = tuple(jnp.tile(a, (bs_, 1)) for a in _rope_tables(past + jnp.arange(ss)))
    cache_kt = jnp.transpose(cache_da_k, (0, 1, 3, 4, 5, 2)).reshape(depth, bs_, DA_QK_COLS, past)
    hist_p = jnp.zeros((bp, CONV_W - 1, D_FF), F32)
    mem_flat = mem_prompt.reshape(bp * MEM_LEN, D_MODEL)

    kt_all = lax.empty((depth, bp, DA_QK_COLS, sp), F32)
    v_all = lax.empty((depth, bp, sp, DA_HEADS, DA_DV), F32)
    mk_all = lax.empty((depth, bp, MEM_LEN, X_HEADS, X_DH), F32)
    mv_all = lax.empty((depth, bp, MEM_LEN, X_HEADS, X_DH), F32)

    xp = x_prompt.reshape(bp * sp, D_MODEL)
    xs = x_sample.reshape(tm_s, D_MODEL)
    outs = {n: [] for n in ('fc_p', 'dk_s', 'dv_s', 'gv_s', 'fc_s')}
    row2 = lambda a: a.reshape(1, -1)
    for l in range(depth):
        lam_init = 0.8 - 0.6 * math.exp(-0.3 * l)
        w = dict(gm_w_s=gm_w_s[l], gm_b=gm_b[l], w_out=w_out[l].astype(BF16),
                 norm_x_g=norm_x_g[l], wq_c=wq_c[l].astype(BF16), wo_c=wo_c[l].astype(BF16),
                 xq_norm_g=xq_norm_g[l], norm_ffn_g=norm_ffn_g[l], w_up=w_up[l].astype(BF16),
                 conv_w=conv_w[l], conv_b=conv_b[l], w_down=w_down[l].astype(BF16))
        w_in_l = w_in[l].astype(BF16)
        w_main = jnp.concatenate([w_in_l[:, :DA_QK_COLS], w_in_l[:, 2 * DA_QK_COLS:]], axis=1)
        w_kt = w_in_l[:, DA_QK_COLS:2 * DA_QK_COLS].T
        qg = jnp.tile(da_q_norm_g[l], 2).reshape(1, -1)
        kg = jnp.tile(da_k_norm_g[l], 2).reshape(1, -1)
        kg_col = jnp.broadcast_to(da_k_norm_g[l][:, None], (DA_DK, SUB_ROWS))
        gmg = jnp.tile(gm_norm_g[l], GM_GROUPS).reshape(1, -1)
        lams = [row2(a[l]) for a in (lambda_q1, lambda_k1, lambda_q2, lambda_k2)]
        subg = row2(da_subln_g[l])
        g_mix = row2(norm_mix_g[l])

        mk_all, mv_all, mkb, mvb = _mem_kv(mem_flat, row2(norm_mem_g[l]), wk_c[l].astype(BF16),
                                           wv_c[l].astype(BF16), row2(xk_norm_g[l]), mk_all, mv_all, l,
                                           tm_mem, f"mem_kv_{l}")
        q, kt_all, v_all, vb, u, gv = _mix_in_prompt(xp, g_mix, w_main, w_kt, qg, kg_col, gmg, rope_p, rope_kp,
                                                     kt_all, v_all, l, sp, tm_w, f"mix_in_p{l}")
        o = _da_prompt(q.reshape(bp, sp, DA_QK_COLS), kt_all, vb.reshape(bp, sp, DA_WIDTH), l, lams, subg,
                       lam_init, f"da_p{l}")
        mem_spec_p = pl.BlockSpec((MEM_LEN, D_MODEL), lambda i: (i // (sp // tm_w), 0))
        xp, cst = _post_attention(xp, o, u, gv, w, mkb, mvb, mem_spec_p, hist_p, sp, tm_p, tm_w, f"p{l}")
        outs['fc_p'].append(cst)

        q, k, v, u, gv = _mix_in_sample(xs, g_mix, w_in_l, qg, kg, gmg, rope_s, f"mix_in_s{l}")
        o = _da_sample(q.reshape(bs_, ss, DA_QK_COLS), k.reshape(bs_, ss, DA_QK_COLS),
                       v.reshape(bs_, ss, DA_WIDTH), cache_kt, cache_da_v, l, lams, subg, lam_init,
                       f"da_s{l}")
        mem_spec_s = pl.BlockSpec((None, None, MEM_LEN, X_HEADS, X_DH), lambda i, l=l: (l, i, 0, 0, 0))
        xs, cst = _post_attention(xs, o, u, gv, w, cache_mem_k, cache_mem_v, mem_spec_s, state_ffn_conv[l],
                                  ss, tm_s, tm_s, f"s{l}")
        outs['dk_s'].append(k.reshape(bs_, ss, DA_HEADS, 2, DA_DK))
        outs['dv_s'].append(v.reshape(bs_, ss, DA_HEADS, DA_DV))
        outs['gv_s'].append(gv.reshape(bs_, ss, GM_GROUPS, GM_CH))
        outs['fc_s'].append(cst)

    st = {n: jnp.stack(a) for n, a in outs.items()}
    dk_p = jnp.transpose(kt_all.reshape(depth, bp, DA_HEADS, 2, DA_DK, sp), (0, 1, 5, 2, 3, 4))
    return (xp.reshape(bp, sp, D_MODEL), xs.reshape(bs_, ss, D_MODEL), dk_p, v_all, mk_all, mv_all,
            st['fc_p'], st['dk_s'], st['dv_s'], st['gv_s'], st['fc_s'])
```

```python
import functools
import math

import numpy as np
import jax
import jax.numpy as jnp
from jax import lax
from jax.experimental import pallas as pl
from jax.experimental.pallas import tpu as pltpu

F32 = jnp.float32
BF16 = jnp.bfloat16

D_MODEL = 1024
CHUNK = 64
DA_HEADS = 4
DA_DK = 64
DA_DV = 128
DA_QK_COLS = DA_HEADS * 2 * DA_DK
DA_WIDTH = DA_HEADS * DA_DV
ROT_DIM = 16
ROT_HALF = ROT_DIM // 2
ROPE_THETA = 500000.0
GM_GROUPS = 4
GM_CH = 128
GM_WIDTH = GM_GROUPS * GM_CH
GM_CHUNK = 128
MEM_LEN = 256
X_HEADS = 4
X_DH = 256
D_FF = 2816
CONV_W = 3
EPS = 1e-6
IN_COLS = 2 * DA_QK_COLS + DA_WIDTH + 2 * GM_WIDTH
NEG = float(np.finfo(np.float32).min)
Q_SCALE = (DA_DK ** -0.5) * math.log2(math.e)

LANES = 128
SUBLANES = 8
VMEM_LIMIT_BYTES = 56 * 1024 * 1024
FF_CHUNK = 256
TOKEN_TILE = 512
WIDE_TILE = 1024
Q_TILE = 256
SUB_ROWS = 512

_NT = (((1,), (1,)), ((), ()))


def _params(sem):
    return pltpu.CompilerParams(dimension_semantics=sem, vmem_limit_bytes=VMEM_LIMIT_BYTES)


def _const_spec(shape):
    n = len(shape)
    return pl.BlockSpec(shape, lambda *_: (0,) * n)


_ANY = pl.BlockSpec(memory_space=pl.ANY)


def _rms(x, g):
    return (x * lax.rsqrt(jnp.mean(x * x, axis=-1, keepdims=True) + EPS)) * g


def _gelu(x):
    return x * (0.5 * (1.0 + jnp.tanh(0.7978845608028654 * (x + 0.044715 * (x * x * x)))))


def _qk_slab(t, gain, cosv, sa, sb):
    lo = lax.broadcasted_iota(jnp.int32, (1, LANES), 1) < DA_DK
    sq = t * t
    s_lo = jnp.sum(jnp.where(lo, sq, 0.0), axis=-1, keepdims=True)
    s_hi = jnp.sum(jnp.where(lo, 0.0, sq), axis=-1, keepdims=True)
    ms = jnp.where(lo, s_lo, s_hi) * (1.0 / DA_DK)
    y = (t * lax.rsqrt(ms + EPS)) * gain
    return y * cosv + pltpu.roll(y, LANES - ROT_HALF, 1) * sa + pltpu.roll(y, ROT_HALF, 1) * sb


def _gmlp_cols(z, c0, gmg_ref, u_ref, gv_ref, rows=slice(None)):
    u_ref[rows, :] = _gelu(z[:, c0:c0 + GM_WIDTH]).astype(u_ref.dtype)
    c1 = c0 + GM_WIDTH
    for s in range(GM_GROUPS):
        cols = slice(s * GM_CH, (s + 1) * GM_CH)
        t = _gelu(z[:, c1 + s * GM_CH:c1 + (s + 1) * GM_CH])
        gv_ref[rows, cols] = _rms(t, gmg_ref[:, cols]).astype(gv_ref.dtype)


def _mix_in_sample_body(x_ref, g_ref, w_ref, qg_ref, kg_ref, gmg_ref, cos_ref, sa_ref, sb_ref,
                        q_ref, k_ref, v_ref, u_ref, gv_ref):
    h = _rms(x_ref[...], g_ref[...]).astype(BF16)
    z = jnp.dot(h, w_ref[...], preferred_element_type=F32)
    cosv, sa, sb = cos_ref[...], sa_ref[...], sb_ref[...]
    for s in range(DA_HEADS):
        cols = slice(s * LANES, (s + 1) * LANES)
        q_ref[:, cols] = (_qk_slab(z[:, cols], qg_ref[...], cosv, sa, sb) * Q_SCALE).astype(q_ref.dtype)
        kc = slice(DA_QK_COLS + s * LANES, DA_QK_COLS + (s + 1) * LANES)
        k_ref[:, cols] = _qk_slab(z[:, kc], kg_ref[...], cosv, sa, sb)
    c2 = 2 * DA_QK_COLS
    v_ref[...] = z[:, c2:c2 + DA_WIDTH]
    _gmlp_cols(z, c2 + DA_WIDTH, gmg_ref, u_ref, gv_ref)


def _mix_in_sample(x, g, w, qg, kg, gmg, rope, name):
    t = x.shape[0]
    row = pl.BlockSpec((t, DA_QK_COLS), lambda i: (0, 0))
    rope_spec = _const_spec((t, LANES))
    return pl.pallas_call(
        _mix_in_sample_body,
        grid=(1,),
        in_specs=[_const_spec((t, D_MODEL)), _const_spec((1, D_MODEL)), _const_spec((D_MODEL, IN_COLS)),
                  _const_spec((1, LANES)), _const_spec((1, LANES)), _const_spec((1, GM_WIDTH)),
                  rope_spec, rope_spec, rope_spec],
        out_specs=[row] * 5,
        out_shape=[jax.ShapeDtypeStruct((t, DA_QK_COLS), BF16),
                   jax.ShapeDtypeStruct((t, DA_QK_COLS), F32),
                   jax.ShapeDtypeStruct((t, DA_WIDTH), F32),
                   jax.ShapeDtypeStruct((t, GM_WIDTH), BF16),
                   jax.ShapeDtypeStruct((t, GM_WIDTH), F32)],
        compiler_params=_params(("arbitrary",)),
        name=name,
    )(x, g, w, qg, kg, gmg, *rope)


def _mix_in_prompt_body(x_ref, g_ref, w_ref, qg_ref, kg_ref, gmg_ref, cos_ref, sa_ref, sb_ref,
                        cosk_ref, sink_ref, kt_in, v_in, q_ref, kt_ref, v_ref, vb_ref, u_ref, gv_ref):
    del kt_in, v_in
    tm = x_ref.shape[0]
    c_k, c_v = DA_QK_COLS, 2 * DA_QK_COLS

    def project(j):
        rows = slice(j * SUB_ROWS, (j + 1) * SUB_ROWS)
        h = _rms(x_ref[rows, :], g_ref[...]).astype(BF16)
        return jnp.dot(h, w_ref[...], preferred_element_type=F32)

    def finish(j, z):
        rows = slice(j * SUB_ROWS, (j + 1) * SUB_ROWS)
        cosv, sa, sb = cos_ref[rows, :], sa_ref[rows, :], sb_ref[rows, :]
        for s in range(DA_HEADS):
            cols = slice(s * LANES, (s + 1) * LANES)
            q_ref[rows, cols] = (_qk_slab(z[:, cols], qg_ref[...], cosv, sa, sb) * Q_SCALE).astype(q_ref.dtype)
            v_ref[pl.ds(j * SUB_ROWS * DA_HEADS + s, SUB_ROWS, stride=DA_HEADS), :] = (
                z[:, c_v + s * DA_DV:c_v + (s + 1) * DA_DV])
        vb_ref[rows, :] = z[:, c_v:c_v + DA_WIDTH].astype(vb_ref.dtype)
        zk = z[:, c_k:c_k + DA_QK_COLS].T
        ck, sk, kg = cosk_ref[:, rows], sink_ref[:, rows], kg_ref[...]
        for grp in range(2 * DA_HEADS):
            r0 = grp * DA_DK
            t = zk[r0:r0 + DA_DK, :]
            inv = lax.rsqrt(jnp.sum(t * t, axis=0, keepdims=True) * (1.0 / DA_DK) + EPS)
            y = (t * inv) * kg
            ya, yb = y[:ROT_HALF], y[ROT_HALF:ROT_DIM]
            kt_ref[r0:r0 + ROT_HALF, rows] = ya * ck - yb * sk
            kt_ref[r0 + ROT_HALF:r0 + ROT_DIM, rows] = yb * ck + ya * sk
            kt_ref[r0 + ROT_DIM:r0 + DA_DK, rows] = y[ROT_DIM:]
        _gmlp_cols(z, c_v + DA_WIDTH, gmg_ref, u_ref, gv_ref, rows)

    n_sub = tm // SUB_ROWS
    nxt = project(0)
    for j in range(n_sub):
        cur = nxt
        if j + 1 < n_sub:
            nxt = project(j + 1)
        finish(j, cur)


def _mix_in_prompt(x, g, w, qg, kg_col, gmg, rope, rope_k, kt_all, v_all, layer, seq, tm, name):
    t = x.shape[0]
    per_b = seq // tm
    row = lambda i: (i, 0)
    rope_spec = pl.BlockSpec((tm, LANES), lambda i: (i % per_b, 0))
    ropek_spec = pl.BlockSpec((SUBLANES, tm), lambda i: (0, i % per_b))
    half = pl.BlockSpec((tm, DA_QK_COLS), row)
    kt_spec = pl.BlockSpec((None, None, DA_QK_COLS, tm), lambda i: (layer, i // per_b, 0, i % per_b))
    v_spec = pl.BlockSpec((tm * DA_HEADS, DA_DV), lambda i: (layer * (t // tm) + i, 0))
    return pl.pallas_call(
        _mix_in_prompt_body,
        grid=(t // tm,),
        in_specs=[pl.BlockSpec((tm, D_MODEL), row), _const_spec((1, D_MODEL)),
                  _const_spec((D_MODEL, IN_COLS)),
                  _const_spec((1, LANES)), _const_spec((DA_DK, SUB_ROWS)), _const_spec((1, GM_WIDTH)),
                  rope_spec, rope_spec, rope_spec, ropek_spec, ropek_spec, _ANY, _ANY],
        out_specs=[half, kt_spec, v_spec, half, half, half],
        out_shape=[jax.ShapeDtypeStruct((t, DA_QK_COLS), BF16),
                   jax.ShapeDtypeStruct(kt_all.shape, F32),
                   jax.ShapeDtypeStruct(v_all.shape, F32),
                   jax.ShapeDtypeStruct((t, DA_WIDTH), BF16),
                   jax.ShapeDtypeStruct((t, GM_WIDTH), BF16),
                   jax.ShapeDtypeStruct((t, GM_WIDTH), BF16)],
        input_output_aliases={11: 1, 12: 2},
        compiler_params=_params(("parallel",)),
        name=name,
    )(x, g, w, qg, kg_col, gmg, *rope, *rope_k, kt_all, v_all)


def _lambda(lq1, lk1, lq2, lk2, lam_init):
    a = jnp.sum(lq1[...] * lk1[...], axis=-1, keepdims=True)
    b = jnp.sum(lq2[...] * lk2[...], axis=-1, keepdims=True)
    return jnp.exp(a) - jnp.exp(b) + lam_init


def _subln(o, g, lam_init):
    return _rms(o, g) * (1.0 - lam_init)


def _split_maps(q):
    lo = lax.broadcasted_iota(jnp.int32, (1, LANES), 1) < DA_DK
    zero = jnp.zeros_like(q)
    return jnp.where(lo, q, zero), jnp.where(lo, zero, q)


def _da_prompt_body(lq1, lk1, lq2, lk2, subg_ref, q_ref, kt_ref, v_ref, o_ref, *, lam_init, bq):
    seq = q_ref.shape[1]
    lam = _lambda(lq1, lk1, lq2, lk2, lam_init)
    qm = _split_maps(q_ref[0])
    kt = kt_ref[...].astype(BF16)
    va = v_ref[0]
    mask = ((lax.broadcasted_iota(jnp.int32, (bq, bq), 1) // CHUNK)
            <= (lax.broadcasted_iota(jnp.int32, (bq, bq), 0) // CHUNK))
    def scores(qi):
        lo, hi = qi * bq, (qi + 1) * bq
        out = []
        for m in range(2):
            q = qm[m][lo:hi]
            sd = jnp.dot(q, kt[:, lo:hi], preferred_element_type=F32)
            so = jnp.dot(q, kt[:, :lo], preferred_element_type=F32) if qi > 0 else None
            out.append((sd, so))
        return out

    nq = seq // bq
    nxt = scores(0)
    for qi in range(nq):
        lo, hi = qi * bq, (qi + 1) * bq
        cur = nxt
        if qi + 1 < nq:
            nxt = scores(qi + 1)
        probs = []
        for m in range(2):
            sd, so = cur[m]
            sd = jnp.where(mask, sd, NEG)
            mx = jnp.max(sd, axis=-1, keepdims=True)
            if qi > 0:
                mx = jnp.maximum(mx, jnp.max(so, axis=-1, keepdims=True))
            pd = jnp.exp2(sd - mx)
            den = jnp.sum(pd, axis=-1, keepdims=True)
            po = None
            if qi > 0:
                po = jnp.exp2(so - mx)
                den = den + jnp.sum(po, axis=-1, keepdims=True)
            probs.append((pd, po, den))
        c1 = 1.0 / probs[0][2]
        r = lam * probs[0][2] / probs[1][2]
        o = jnp.dot((probs[0][0] - r * probs[1][0]).astype(BF16), va[lo:hi], preferred_element_type=F32)
        if qi > 0:
            o = o + jnp.dot((probs[0][1] - r * probs[1][1]).astype(BF16), va[:lo],
                            preferred_element_type=F32)
        o_ref[0, lo:hi, :] = _subln(o * c1, subg_ref[...], lam_init).astype(o_ref.dtype)


def _da_prompt(q, kt_all, vb, layer, lams, subg, lam_init, name):
    b, s, _ = q.shape
    lam_spec = _const_spec((1, DA_DK))
    q_spec = pl.BlockSpec((1, s, LANES), lambda bi, h: (bi, 0, h))
    kt_spec = pl.BlockSpec((None, None, 2 * DA_DK, s), lambda bi, h: (layer, bi, h, 0))
    return pl.pallas_call(
        functools.partial(_da_prompt_body, lam_init=lam_init, bq=Q_TILE),
        grid=(b, DA_HEADS),
        in_specs=[lam_spec, lam_spec, lam_spec, lam_spec, _const_spec((1, DA_DV)),
                  q_spec, kt_spec, q_spec],
        out_specs=q_spec,
        out_shape=jax.ShapeDtypeStruct((b, s, DA_WIDTH), BF16),
        compiler_params=_params(("parallel", "parallel")),
        name=name,
    )(*lams, subg, q, kt_all, vb)


def _da_sample_body(lq1, lk1, lq2, lk2, subg_ref, q_ref, ktp_ref, kn_ref, vp_ref, vn_ref, o_ref,
                    *, lam_init, past, sq):
    lam = _lambda(lq1, lk1, lq2, lk2, lam_init)
    q_chunk = (past + lax.broadcasted_iota(jnp.int32, (sq, 1), 0)) // CHUNK
    mask_p = (lax.broadcasted_iota(jnp.int32, (1, past), 1) // CHUNK) <= q_chunk
    mask_n = ((past + lax.broadcasted_iota(jnp.int32, (1, sq), 1)) // CHUNK) <= q_chunk
    for h in range(DA_HEADS):
        cols = slice(h * LANES, (h + 1) * LANES)
        qm = _split_maps(q_ref[0, :, cols])
        ktp = ktp_ref[cols, :].astype(BF16)
        kn = kn_ref[0, :, cols].astype(BF16)
        probs = []
        for m in range(2):
            sp = jnp.where(mask_p, jnp.dot(qm[m], ktp, preferred_element_type=F32), NEG)
            sn = jnp.where(mask_n, lax.dot_general(qm[m], kn, _NT, preferred_element_type=F32), NEG)
            mx = jnp.maximum(jnp.max(sp, axis=-1, keepdims=True), jnp.max(sn, axis=-1, keepdims=True))
            pp = jnp.exp2(sp - mx)
            pn = jnp.exp2(sn - mx)
            inv = 1.0 / (jnp.sum(pp, axis=-1, keepdims=True) + jnp.sum(pn, axis=-1, keepdims=True))
            probs.append((pp * inv, pn * inv))
        ap = (probs[0][0] - lam * probs[1][0]).astype(BF16)
        an = (probs[0][1] - lam * probs[1][1]).astype(BF16)
        vp = vp_ref[pl.ds(h, past, stride=DA_HEADS), :]
        o = (jnp.dot(ap, vp.astype(BF16), preferred_element_type=F32)
             + jnp.dot(an, vn_ref[0, :, cols].astype(BF16), preferred_element_type=F32))
        o_ref[0, :, cols] = _subln(o, subg_ref[...], lam_init).astype(o_ref.dtype)


def _da_sample(q, k_new, v_new, cache_kt, cache_v, layer, lams, subg, lam_init, name):
    b, sq, _ = q.shape
    past = cache_kt.shape[3]
    lam_spec = _const_spec((1, DA_DK))
    new_spec = pl.BlockSpec((1, sq, DA_QK_COLS), lambda bi: (bi, 0, 0))
    ktp_spec = pl.BlockSpec((None, None, DA_QK_COLS, past), lambda bi: (layer, bi, 0, 0))
    vp_spec = pl.BlockSpec((past * DA_HEADS, DA_DV), lambda bi: (layer * b + bi, 0))
    return pl.pallas_call(
        functools.partial(_da_sample_body, lam_init=lam_init, past=past, sq=sq),
        grid=(b,),
        in_specs=[lam_spec, lam_spec, lam_spec, lam_spec, _const_spec((1, DA_DV)),
                  new_spec, ktp_spec, new_spec, vp_spec, new_spec],
        out_specs=new_spec,
        out_shape=jax.ShapeDtypeStruct((b, sq, DA_WIDTH), BF16),
        compiler_params=_params(("parallel",)),
        name=name,
    )(*lams, subg, q, cache_kt, k_new, cache_v, v_new)


def _mem_kv_body(mem_ref, g_ref, wk_ref, wv_ref, kg_ref, k_in, v_in, k_ref, v_ref, kb_ref, vb_ref, *, nb):
    del k_in, v_in
    m = _rms(mem_ref[...], g_ref[...]).astype(BF16)
    k = jnp.dot(m, wk_ref[...], preferred_element_type=F32)
    v = jnp.dot(m, wv_ref[...], preferred_element_type=F32)
    vb_ref[...] = v.astype(vb_ref.dtype)
    for h in range(X_HEADS):
        cols = slice(h * X_DH, (h + 1) * X_DH)
        kn = _rms(k[:, cols], kg_ref[...])
        kb_ref[:, cols] = kn.astype(kb_ref.dtype)
        for bi in range(nb):
            rows = slice(bi * MEM_LEN, (bi + 1) * MEM_LEN)
            k_ref[bi, :, h, :] = kn[rows]
            v_ref[bi, :, h, :] = v[rows, cols]


def _mem_kv(mem, g, wk, wv, kg, mk_all, mv_all, layer, tm, name):
    t = mem.shape[0]
    nb = tm // MEM_LEN
    blk = pl.BlockSpec((tm, D_MODEL), lambda i: (i, 0))
    out_spec = pl.BlockSpec((None, nb, MEM_LEN, X_HEADS, X_DH), lambda i: (layer, i, 0, 0, 0))
    return pl.pallas_call(
        functools.partial(_mem_kv_body, nb=nb),
        grid=(t // tm,),
        in_specs=[blk, _const_spec((1, D_MODEL)), _const_spec((D_MODEL, D_MODEL)),
                  _const_spec((D_MODEL, D_MODEL)), _const_spec((1, X_DH)), _ANY, _ANY],
        out_specs=[out_spec, out_spec, blk, blk],
        out_shape=[jax.ShapeDtypeStruct(mk_all.shape, F32), jax.ShapeDtypeStruct(mv_all.shape, F32),
                   jax.ShapeDtypeStruct((t, D_MODEL), BF16), jax.ShapeDtypeStruct((t, D_MODEL), BF16)],
        input_output_aliases={5: 0, 6: 1},
        compiler_params=_params(("parallel",)),
        name=name,
    )(mem, g, wk, wv, kg, mk_all, mv_all)


def _mix_xattn_body(x_ref, o_ref, u_ref, gv_ref, ws_ref, bs_ref, wout_ref, g_ref, wq_ref, qg_ref,
                    mk_ref, mv_ref, wo_ref, xo_ref, cat_ref, att_ref, *, chunk):
    tm = x_ref.shape[0]
    sub = min(SUB_ROWS, tm)
    flat = len(mk_ref.shape) == 2
    tri = (lax.broadcasted_iota(jnp.int32, (chunk, chunk), 1)
           <= lax.broadcasted_iota(jnp.int32, (chunk, chunk), 0))
    ws = [jnp.where(tri, ws_ref[g], 0.0).astype(BF16) for g in range(GM_GROUPS)]

    def mix_out(j):
        r0 = j * sub
        rows = slice(r0, r0 + sub)
        cat_ref[rows, :DA_WIDTH] = o_ref[rows, :]
        for g in range(GM_GROUPS):
            cols = slice(g * GM_CH, (g + 1) * GM_CH)
            bias = bs_ref[g]
            for c in range(sub // chunk):
                cr = slice(r0 + c * chunk, r0 + (c + 1) * chunk)
                s = jnp.dot(ws[g], gv_ref[cr, cols].astype(BF16), preferred_element_type=F32) + bias
                cat_ref[cr, DA_WIDTH + g * GM_CH:DA_WIDTH + (g + 1) * GM_CH] = (
                    u_ref[cr, cols].astype(F32) * s).astype(BF16)
        x1 = x_ref[rows, :] + jnp.dot(cat_ref[rows, :], wout_ref[...], preferred_element_type=F32)
        h = _rms(x1, g_ref[...]).astype(BF16)
        return x1, jnp.dot(h, wq_ref[...], preferred_element_type=F32)

    def attend(j, x1, q):
        rows = slice(j * sub, (j + 1) * sub)
        for hh in range(X_HEADS):
            cols = slice(hh * X_DH, (hh + 1) * X_DH)
            mk = mk_ref[:, cols] if flat else mk_ref[:, hh, :]
            mv = mv_ref[:, cols] if flat else mv_ref[:, hh, :]
            qn = (_rms(q[:, cols], qg_ref[...]) * (X_DH ** -0.5)).astype(BF16)
            s = lax.dot_general(qn, mk.astype(BF16), _NT, preferred_element_type=F32)
            p = jnp.exp(s - jnp.max(s, axis=-1, keepdims=True))
            p = p * (1.0 / jnp.sum(p, axis=-1, keepdims=True))
            att_ref[rows, cols] = jnp.dot(p.astype(BF16), mv.astype(BF16),
                                          preferred_element_type=F32).astype(BF16)
        xo_ref[rows, :] = x1 + jnp.dot(att_ref[rows, :], wo_ref[...], preferred_element_type=F32)

    n_sub = tm // sub
    nxt = mix_out(0)
    for j in range(n_sub):
        cur = nxt
        if j + 1 < n_sub:
            nxt = mix_out(j + 1)
        attend(j, *cur)


def _mix_xattn(x, o, u, gv, ws, bs, wout, g, wq, qg, mk, mv, mem_spec, wo, tm, chunk, name):
    t = x.shape[0]
    row = lambda i: (i, 0)
    blk = pl.BlockSpec((tm, D_MODEL), row)
    half = pl.BlockSpec((tm, DA_WIDTH), row)
    sq = _const_spec((D_MODEL, D_MODEL))
    return pl.pallas_call(
        functools.partial(_mix_xattn_body, chunk=chunk),
        grid=(t // tm,),
        in_specs=[blk, half, half, half, _const_spec((GM_GROUPS, chunk, chunk)),
                  _const_spec((GM_GROUPS, chunk, GM_CH)), sq, _const_spec((1, D_MODEL)), sq,
                  _const_spec((1, X_DH)), mem_spec, mem_spec, sq],
        out_specs=blk,
        out_shape=jax.ShapeDtypeStruct((t, D_MODEL), F32),
        scratch_shapes=[pltpu.VMEM((tm, D_MODEL), BF16), pltpu.VMEM((tm, D_MODEL), BF16)],
        compiler_params=_params(("parallel",)),
        name=name,
    )(x, o, u, gv, ws, bs, wout, g, wq, qg, mk, mv, wo)


def _conv_ffn_body(x_ref, g_ref, wup_ref, cw_ref, cb_ref, wdn_ref, hist_ref, xo_ref, cst_ref,
                   gp_ref, carry_ref, act_ref, *, nb, sub, tiles_per_b):
    i = pl.program_id(0)
    tm = x_ref.shape[0]
    n_sub = tm // sub
    r = sub // nb
    chained = tiles_per_b > 1 or n_sub > 1
    first_tile = (i % tiles_per_b) == 0
    n_chunks = D_FF // FF_CHUNK

    def gate(j):
        h = _rms(x_ref[j * sub:(j + 1) * sub, :], g_ref[...]).astype(BF16)
        for c in range(n_chunks):
            cs = slice(c * FF_CHUNK, (c + 1) * FF_CHUNK)
            g = jnp.dot(h, wup_ref[:, cs], preferred_element_type=F32)
            up = jnp.dot(h, wup_ref[:, D_FF + c * FF_CHUNK:D_FF + (c + 1) * FF_CHUNK],
                         preferred_element_type=F32)
            for bi in range(nb):
                lrows = slice(bi * r, (bi + 1) * r)
                rows = slice(j * sub + bi * r, j * sub + (bi + 1) * r)
                if not chained:
                    prev = hist_ref[bi, :, cs]
                elif j == 0:
                    prev = jnp.where(first_tile, hist_ref[bi, :, cs], carry_ref[:, cs])
                else:
                    prev = carry_ref[:, cs]
                gp = gp_ref.at[c]
                gp[6:8, :] = prev
                gp[8:8 + r, :] = g[lrows]
                conv = (cb_ref[:, cs] + cw_ref[0:1, cs] * gp[6:6 + r, :]
                        + cw_ref[1:2, cs] * gp[7:7 + r, :] + cw_ref[2:3, cs] * g[lrows])
                act = conv * (1.0 / (1.0 + jnp.exp(-conv))) * up[lrows]
                act_ref[rows, cs] = act.astype(BF16)
                last2 = gp[6 + r:8 + r, :]
                cst_ref[bi, :, cs] = last2
                if chained:
                    carry_ref[:, cs] = last2

    def down(j):
        rows = slice(j * sub, (j + 1) * sub)
        xo_ref[rows, :] = x_ref[rows, :] + jnp.dot(act_ref[rows, :], wdn_ref[...],
                                                   preferred_element_type=F32)

    gate(0)
    for j in range(n_sub):
        if j + 1 < n_sub:
            gate(j + 1)
        down(j)


def _conv_ffn(x, g, wup, cw, cb, wdn, hist, tm, sub, nb, tiles_per_b, name):
    t = x.shape[0]
    row = lambda i: (i, 0)
    blk = pl.BlockSpec((tm, D_MODEL), row)
    hist_spec = pl.BlockSpec((nb, CONV_W - 1, D_FF), lambda i: (i // tiles_per_b, 0, 0))
    resident = lambda shape: pl.BlockSpec(shape, lambda i: (0, 0), pipeline_mode=pl.Buffered(1))
    return pl.pallas_call(
        functools.partial(_conv_ffn_body, nb=nb, sub=sub, tiles_per_b=tiles_per_b),
        grid=(t // tm,),
        in_specs=[blk, _const_spec((1, D_MODEL)), resident((D_MODEL, 2 * D_FF)),
                  _const_spec((CONV_W, D_FF)), _const_spec((1, D_FF)), resident((D_FF, D_MODEL)),
                  hist_spec],
        out_specs=[blk, hist_spec],
        out_shape=[jax.ShapeDtypeStruct((t, D_MODEL), F32),
                   jax.ShapeDtypeStruct(hist.shape, F32)],
        scratch_shapes=[pltpu.VMEM((D_FF // FF_CHUNK, sub // nb + 8, FF_CHUNK), F32),
                        pltpu.VMEM((CONV_W - 1, D_FF), F32),
                        pltpu.VMEM((tm, D_FF), BF16)],
        compiler_params=_params(("arbitrary",)),
        name=name,
    )(x, g, wup, cw, cb, wdn, hist)


def _rope_angles(pos):
    inv = ROPE_THETA ** (-jnp.arange(ROT_HALF, dtype=F32) / ROT_HALF)
    ang = pos.astype(F32)[:, None] * inv[None, :]
    return jnp.cos(ang), jnp.sin(ang)


def _rope_tables(pos):
    cos, sin = _rope_angles(pos)
    n = pos.shape[0]
    ones = jnp.ones((n, DA_DK - ROT_DIM), F32)
    zeros_h = jnp.zeros((n, ROT_HALF), F32)
    zeros_r = jnp.zeros((n, DA_DK - ROT_DIM), F32)
    cos64 = jnp.concatenate([cos, cos, ones], axis=1)
    sa64 = jnp.concatenate([-sin, zeros_h, zeros_r], axis=1)
    sb64 = jnp.concatenate([zeros_h, sin, zeros_r], axis=1)
    two = lambda a: jnp.concatenate([a, a], axis=1)
    return two(cos64), two(sa64), two(sb64)


def _post_attention(x, o, u, gv, w, mk, mv, mem_spec, hist, seq, tm, tag):
    t = x.shape[0]
    row2 = lambda a: a.reshape(1, -1)
    chunk = min(seq, GM_CHUNK)
    ws = w['gm_w_s'][:, :chunk, :chunk]
    bs = jnp.broadcast_to(w['gm_b'][:, :chunk, None], (GM_GROUPS, chunk, GM_CH))
    x = _mix_xattn(x, o.reshape(t, DA_WIDTH), u, gv, ws, bs, w['w_out'], row2(w['norm_x_g']), w['wq_c'],
                   row2(w['xq_norm_g']), mk, mv, mem_spec, w['wo_c'], min(tm, seq), chunk,
                   f"mix_xattn_{tag}")
    sub = min(SUB_ROWS, tm)
    return _conv_ffn(x, row2(w['norm_ffn_g']), w['w_up'], w['conv_w'], row2(w['conv_b']),
                     w['w_down'], hist, tm, sub, max(1, sub // seq), max(1, seq // tm), f"conv_ffn_{tag}")


def kernel(x_prompt, x_sample, cache_da_k, cache_da_v, cache_mem_k, cache_mem_v, state_ffn_conv, mem_prompt, norm_mix_g, w_in, da_q_norm_g, da_k_norm_g, lambda_q1, lambda_k1, lambda_q2, lambda_k2, da_subln_g, gm_norm_g, gm_w_s, gm_b, w_out, norm_x_g, norm_mem_g, wq_c, wk_c, wv_c, wo_c, xq_norm_g, xk_norm_g, norm_ffn_g, w_up, conv_w, conv_b, w_down):
    bp, sp, _ = x_prompt.shape
    bs_, ss, _ = x_sample.shape
    depth = w_in.shape[0]
    past = cache_da_k.shape[2]
    tm_w = min(WIDE_TILE, sp)
    tm_s = bs_ * ss
    tm_mem = min(TOKEN_TILE, bp * MEM_LEN)
    assert sp % tm_w == 0 and tm_w % SUB_ROWS == 0 and sp % Q_TILE == 0 and SUB_ROWS % GM_CHUNK == 0
    assert ss <= GM_CHUNK and tm_s % SUBLANES == 0 and ss >= CONV_W - 1 and ss % SUBLANES == 0
    assert (bp * MEM_LEN) % tm_mem == 0 and tm_mem % MEM_LEN == 0

    pos_p = jnp.arange(sp)
    rope_p = _rope_tables(pos_p)
    cos_p, sin_p = _rope_angles(pos_p)
    rope_kp = (cos_p.T, sin_p.T)
    rope_s = tuple(jnp.tile(a, (bs_, 1)) for a in _rope_tables(past + jnp.arange(ss)))
    cache_kt = jnp.transpose(cache_da_k, (0, 1, 3, 4, 5, 2)).reshape(depth, bs_, DA_QK_COLS, past)
    cache_v2 = cache_da_v.reshape(depth * bs_ * past * DA_HEADS, DA_DV)
    hist_p = jnp.zeros((bp, CONV_W - 1, D_FF), F32)
    mem_flat = mem_prompt.reshape(bp * MEM_LEN, D_MODEL)

    kt_all = lax.empty((depth, bp, DA_QK_COLS, sp), F32)
    v_all = lax.empty((depth * bp * sp * DA_HEADS, DA_DV), F32)
    mk_all = lax.empty((depth, bp, MEM_LEN, X_HEADS, X_DH), F32)
    mv_all = lax.empty((depth, bp, MEM_LEN, X_HEADS, X_DH), F32)

    xp = x_prompt.reshape(bp * sp, D_MODEL)
    xs = x_sample.reshape(tm_s, D_MODEL)
    outs = {n: [] for n in ('fc_p', 'dk_s', 'dv_s', 'gv_s', 'fc_s')}
    row2 = lambda a: a.reshape(1, -1)
    for l in range(depth):
        lam_init = 0.8 - 0.6 * math.exp(-0.3 * l)
        w = dict(gm_w_s=gm_w_s[l], gm_b=gm_b[l], w_out=w_out[l].astype(BF16),
                 norm_x_g=norm_x_g[l], wq_c=wq_c[l].astype(BF16), wo_c=wo_c[l].astype(BF16),
                 xq_norm_g=xq_norm_g[l], norm_ffn_g=norm_ffn_g[l], w_up=w_up[l].astype(BF16),
                 conv_w=conv_w[l], conv_b=conv_b[l], w_down=w_down[l].astype(BF16))
        w_in_l = w_in[l].astype(BF16)
        qg = jnp.tile(da_q_norm_g[l], 2).reshape(1, -1)
        kg = jnp.tile(da_k_norm_g[l], 2).reshape(1, -1)
        kg_col = jnp.broadcast_to(da_k_norm_g[l][:, None], (DA_DK, SUB_ROWS))
        gmg = jnp.tile(gm_norm_g[l], GM_GROUPS).reshape(1, -1)
        lams = [row2(a[l]) for a in (lambda_q1, lambda_k1, lambda_q2, lambda_k2)]
        subg = row2(da_subln_g[l])
        g_mix = row2(norm_mix_g[l])

        mk_all, mv_all, mkb, mvb = _mem_kv(mem_flat, row2(norm_mem_g[l]), wk_c[l].astype(BF16),
                                           wv_c[l].astype(BF16), row2(xk_norm_g[l]), mk_all, mv_all, l,
                                           tm_mem, f"mem_kv_{l}")
        q, kt_all, v_all, vb, u, gv = _mix_in_prompt(xp, g_mix, w_in_l, qg, kg_col, gmg, rope_p, rope_kp,
                                                     kt_all, v_all, l, sp, tm_w, f"mix_in_p{l}")
        o = _da_prompt(q.reshape(bp, sp, DA_QK_COLS), kt_all, vb.reshape(bp, sp, DA_WIDTH), l, lams, subg,
                       lam_init, f"da_p{l}")
        mem_spec_p = pl.BlockSpec((MEM_LEN, D_MODEL), lambda i: (i // (sp // tm_w), 0))
        xp, cst = _post_attention(xp, o, u, gv, w, mkb, mvb, mem_spec_p, hist_p, sp, tm_w, f"p{l}")
        outs['fc_p'].append(cst)

        q, k, v, u, gv = _mix_in_sample(xs, g_mix, w_in_l, qg, kg, gmg, rope_s, f"mix_in_s{l}")
        o = _da_sample(q.reshape(bs_, ss, DA_QK_COLS), k.reshape(bs_, ss, DA_QK_COLS),
                       v.reshape(bs_, ss, DA_WIDTH), cache_kt, cache_v2, l, lams, subg, lam_init,
                       f"da_s{l}")
        mem_spec_s = pl.BlockSpec((None, None, MEM_LEN, X_HEADS, X_DH), lambda i, l=l: (l, i, 0, 0, 0))
        xs, cst = _post_attention(xs, o, u, gv, w, cache_mem_k, cache_mem_v, mem_spec_s, state_ffn_conv[l],
                                  ss, tm_s, f"s{l}")
        outs['dk_s'].append(k.reshape(bs_, ss, DA_HEADS, 2, DA_DK))
        outs['dv_s'].append(v.reshape(bs_, ss, DA_HEADS, DA_DV))
        outs['gv_s'].append(gv.reshape(bs_, ss, GM_GROUPS, GM_CH))
        outs['fc_s'].append(cst)

    st = {n: jnp.stack(a) for n, a in outs.items()}
    dk_p = jnp.transpose(kt_all.reshape(depth, bp, DA_HEADS, 2, DA_DK, sp), (0, 1, 5, 2, 3, 4))
    dv_p = v_all.reshape(depth, bp, sp, DA_HEADS, DA_DV)
    return (xp.reshape(bp, sp, D_MODEL), xs.reshape(bs_, ss, D_MODEL), dk_p, dv_p, mk_all, mv_all,
            st['fc_p'], st['dk_s'], st['dv_s'], st['gv_s'], st['fc_s'])
```

```python
import functools
import math

import numpy as np
import jax
import jax.numpy as jnp
from jax import lax
from jax.experimental import pallas as pl
from jax.experimental.pallas import tpu as pltpu

F32 = jnp.float32
BF16 = jnp.bfloat16

D_MODEL = 1024
CHUNK = 64
DA_HEADS = 4
DA_DK = 64
DA_DV = 128
DA_QK_COLS = DA_HEADS * 2 * DA_DK
DA_WIDTH = DA_HEADS * DA_DV
ROT_DIM = 16
ROT_HALF = ROT_DIM // 2
ROPE_THETA = 500000.0
GM_GROUPS = 4
GM_CH = 128
GM_WIDTH = GM_GROUPS * GM_CH
GM_CHUNK = 128
MEM_LEN = 256
X_HEADS = 4
X_DH = 256
D_FF = 2816
CONV_W = 3
EPS = 1e-6
IN_COLS = 2 * DA_QK_COLS + DA_WIDTH + 2 * GM_WIDTH
NEG = float(np.finfo(np.float32).min)
Q_SCALE = (DA_DK ** -0.5) * math.log2(math.e)

LANES = 128
SUBLANES = 8
VMEM_LIMIT_BYTES = 56 * 1024 * 1024
FF_CHUNK = 256
TOKEN_TILE = 512
WIDE_TILE = 1024
Q_TILE = 256
DA_HEADS_PER_STEP = 2
SUB_ROWS = 512

_NT = (((1,), (1,)), ((), ()))


def _params(sem):
    return pltpu.CompilerParams(dimension_semantics=sem, vmem_limit_bytes=VMEM_LIMIT_BYTES)


def _const_spec(shape):
    n = len(shape)
    return pl.BlockSpec(shape, lambda *_: (0,) * n)


_ANY = pl.BlockSpec(memory_space=pl.ANY)


def _rms(x, g):
    return (x * lax.rsqrt(jnp.mean(x * x, axis=-1, keepdims=True) + EPS)) * g


def _gelu(x):
    return x * (0.5 * (1.0 + jnp.tanh(0.7978845608028654 * (x + 0.044715 * (x * x * x)))))


def _qk_slab(t, gain, cosv, sa, sb):
    lo = lax.broadcasted_iota(jnp.int32, (1, LANES), 1) < DA_DK
    sq = t * t
    s_lo = jnp.sum(jnp.where(lo, sq, 0.0), axis=-1, keepdims=True)
    s_hi = jnp.sum(jnp.where(lo, 0.0, sq), axis=-1, keepdims=True)
    ms = jnp.where(lo, s_lo, s_hi) * (1.0 / DA_DK)
    y = (t * lax.rsqrt(ms + EPS)) * gain
    return y * cosv + pltpu.roll(y, LANES - ROT_HALF, 1) * sa + pltpu.roll(y, ROT_HALF, 1) * sb


def _gmlp_cols(z, c0, gmg_ref, u_ref, gv_ref, rows=slice(None)):
    u_ref[rows, :] = _gelu(z[:, c0:c0 + GM_WIDTH]).astype(u_ref.dtype)
    c1 = c0 + GM_WIDTH
    for s in range(GM_GROUPS):
        cols = slice(s * GM_CH, (s + 1) * GM_CH)
        t = _gelu(z[:, c1 + s * GM_CH:c1 + (s + 1) * GM_CH])
        gv_ref[rows, cols] = _rms(t, gmg_ref[:, cols]).astype(gv_ref.dtype)


def _mix_in_sample_body(x_ref, g_ref, w_ref, qg_ref, kg_ref, gmg_ref, cos_ref, sa_ref, sb_ref,
                        q_ref, k_ref, v_ref, u_ref, gv_ref):
    h = _rms(x_ref[...], g_ref[...]).astype(BF16)
    z = jnp.dot(h, w_ref[...], preferred_element_type=F32)
    cosv, sa, sb = cos_ref[...], sa_ref[...], sb_ref[...]
    for s in range(DA_HEADS):
        cols = slice(s * LANES, (s + 1) * LANES)
        q_ref[:, cols] = (_qk_slab(z[:, cols], qg_ref[...], cosv, sa, sb) * Q_SCALE).astype(q_ref.dtype)
        kc = slice(DA_QK_COLS + s * LANES, DA_QK_COLS + (s + 1) * LANES)
        k_ref[:, cols] = _qk_slab(z[:, kc], kg_ref[...], cosv, sa, sb)
    c2 = 2 * DA_QK_COLS
    v_ref[...] = z[:, c2:c2 + DA_WIDTH]
    _gmlp_cols(z, c2 + DA_WIDTH, gmg_ref, u_ref, gv_ref)


def _mix_in_sample(x, g, w, qg, kg, gmg, rope, name):
    t = x.shape[0]
    row = pl.BlockSpec((t, DA_QK_COLS), lambda i: (0, 0))
    rope_spec = _const_spec((t, LANES))
    return pl.pallas_call(
        _mix_in_sample_body,
        grid=(1,),
        in_specs=[_const_spec((t, D_MODEL)), _const_spec((1, D_MODEL)), _const_spec((D_MODEL, IN_COLS)),
                  _const_spec((1, LANES)), _const_spec((1, LANES)), _const_spec((1, GM_WIDTH)),
                  rope_spec, rope_spec, rope_spec],
        out_specs=[row] * 5,
        out_shape=[jax.ShapeDtypeStruct((t, DA_QK_COLS), BF16),
                   jax.ShapeDtypeStruct((t, DA_QK_COLS), F32),
                   jax.ShapeDtypeStruct((t, DA_WIDTH), F32),
                   jax.ShapeDtypeStruct((t, GM_WIDTH), BF16),
                   jax.ShapeDtypeStruct((t, GM_WIDTH), F32)],
        compiler_params=_params(("arbitrary",)),
        name=name,
    )(x, g, w, qg, kg, gmg, *rope)


def _mix_in_prompt_body(x_ref, g_ref, w_ref, qg_ref, kg_ref, gmg_ref, cos_ref, sa_ref, sb_ref,
                        cosk_ref, sink_ref, kt_in, v_in, q_ref, kt_ref, v_ref, vb_ref, u_ref, gv_ref):
    del kt_in, v_in
    tm = x_ref.shape[0]
    c_k, c_v = DA_QK_COLS, 2 * DA_QK_COLS

    def project(j):
        rows = slice(j * SUB_ROWS, (j + 1) * SUB_ROWS)
        h = _rms(x_ref[rows, :], g_ref[...]).astype(BF16)
        return jnp.dot(h, w_ref[...], preferred_element_type=F32)

    def finish(j, z):
        rows = slice(j * SUB_ROWS, (j + 1) * SUB_ROWS)
        cosv, sa, sb = cos_ref[rows, :], sa_ref[rows, :], sb_ref[rows, :]
        for s in range(DA_HEADS):
            cols = slice(s * LANES, (s + 1) * LANES)
            q_ref[rows, cols] = (_qk_slab(z[:, cols], qg_ref[...], cosv, sa, sb) * Q_SCALE).astype(q_ref.dtype)
            v_ref[pl.ds(j * SUB_ROWS * DA_HEADS + s, SUB_ROWS, stride=DA_HEADS), :] = (
                z[:, c_v + s * DA_DV:c_v + (s + 1) * DA_DV])
        vb_ref[rows, :] = z[:, c_v:c_v + DA_WIDTH].astype(vb_ref.dtype)
        zk = z[:, c_k:c_k + DA_QK_COLS].T
        ck, sk, kg = cosk_ref[:, rows], sink_ref[:, rows], kg_ref[...]
        for grp in range(2 * DA_HEADS):
            r0 = grp * DA_DK
            t = zk[r0:r0 + DA_DK, :]
            inv = lax.rsqrt(jnp.sum(t * t, axis=0, keepdims=True) * (1.0 / DA_DK) + EPS)
            y = (t * inv) * kg
            ya, yb = y[:ROT_HALF], y[ROT_HALF:ROT_DIM]
            kt_ref[r0:r0 + ROT_HALF, rows] = ya * ck - yb * sk
            kt_ref[r0 + ROT_HALF:r0 + ROT_DIM, rows] = yb * ck + ya * sk
            kt_ref[r0 + ROT_DIM:r0 + DA_DK, rows] = y[ROT_DIM:]
        _gmlp_cols(z, c_v + DA_WIDTH, gmg_ref, u_ref, gv_ref, rows)

    n_sub = tm // SUB_ROWS
    nxt = project(0)
    for j in range(n_sub):
        cur = nxt
        if j + 1 < n_sub:
            nxt = project(j + 1)
        finish(j, cur)


def _mix_in_prompt(x, g, w, qg, kg_col, gmg, rope, rope_k, kt_all, v_all, layer, seq, tm, name):
    t = x.shape[0]
    per_b = seq // tm
    row = lambda i: (i, 0)
    rope_spec = pl.BlockSpec((tm, LANES), lambda i: (i % per_b, 0))
    ropek_spec = pl.BlockSpec((SUBLANES, tm), lambda i: (0, i % per_b))
    half = pl.BlockSpec((tm, DA_QK_COLS), row)
    kt_spec = pl.BlockSpec((None, None, DA_QK_COLS, tm), lambda i: (layer, i // per_b, 0, i % per_b))
    v_spec = pl.BlockSpec((tm * DA_HEADS, DA_DV), lambda i: (layer * (t // tm) + i, 0))
    return pl.pallas_call(
        _mix_in_prompt_body,
        grid=(t // tm,),
        in_specs=[pl.BlockSpec((tm, D_MODEL), row), _const_spec((1, D_MODEL)),
                  _const_spec((D_MODEL, IN_COLS)),
                  _const_spec((1, LANES)), _const_spec((DA_DK, SUB_ROWS)), _const_spec((1, GM_WIDTH)),
                  rope_spec, rope_spec, rope_spec, ropek_spec, ropek_spec, _ANY, _ANY],
        out_specs=[half, kt_spec, v_spec, half, half, half],
        out_shape=[jax.ShapeDtypeStruct((t, DA_QK_COLS), BF16),
                   jax.ShapeDtypeStruct(kt_all.shape, F32),
                   jax.ShapeDtypeStruct(v_all.shape, F32),
                   jax.ShapeDtypeStruct((t, DA_WIDTH), BF16),
                   jax.ShapeDtypeStruct((t, GM_WIDTH), BF16),
                   jax.ShapeDtypeStruct((t, GM_WIDTH), BF16)],
        input_output_aliases={11: 1, 12: 2},
        compiler_params=_params(("parallel",)),
        name=name,
    )(x, g, w, qg, kg_col, gmg, *rope, *rope_k, kt_all, v_all)


def _lambda(lq1, lk1, lq2, lk2, lam_init):
    a = jnp.sum(lq1[...] * lk1[...], axis=-1, keepdims=True)
    b = jnp.sum(lq2[...] * lk2[...], axis=-1, keepdims=True)
    return jnp.exp(a) - jnp.exp(b) + lam_init


def _subln(o, g, lam_init):
    return _rms(o, g) * (1.0 - lam_init)


def _split_maps(q):
    lo = lax.broadcasted_iota(jnp.int32, (1, LANES), 1) < DA_DK
    zero = jnp.zeros_like(q)
    return jnp.where(lo, q, zero), jnp.where(lo, zero, q)


def _da_prompt_body(lq1, lk1, lq2, lk2, subg_ref, q_ref, kt_ref, v_ref, o_ref, *, lam_init, bq):
    seq = q_ref.shape[1]
    lam = _lambda(lq1, lk1, lq2, lk2, lam_init)
    mask = ((lax.broadcasted_iota(jnp.int32, (bq, bq), 1) // CHUNK)
            <= (lax.broadcasted_iota(jnp.int32, (bq, bq), 0) // CHUNK))
    heads = []
    for h in range(DA_HEADS_PER_STEP):
        cols = slice(h * LANES, (h + 1) * LANES)
        heads.append((_split_maps(q_ref[0, :, cols]),
                      kt_ref[cols, :].astype(BF16),
                      v_ref[0, :, cols]))

    def scores(h, qi):
        qm, kt, _ = heads[h]
        lo, hi = qi * bq, (qi + 1) * bq
        out = []
        for m in range(2):
            q = qm[m][lo:hi]
            sd = jnp.dot(q, kt[:, lo:hi], preferred_element_type=F32)
            so = jnp.dot(q, kt[:, :lo], preferred_element_type=F32) if qi > 0 else None
            out.append((sd, so))
        return out

    def attend(h, qi, cur):
        va = heads[h][2]
        lo, hi = qi * bq, (qi + 1) * bq
        probs = []
        for m in range(2):
            sd, so = cur[m]
            sd = jnp.where(mask, sd, NEG)
            mx = jnp.max(sd, axis=-1, keepdims=True)
            if qi > 0:
                mx = jnp.maximum(mx, jnp.max(so, axis=-1, keepdims=True))
            pd = jnp.exp2(sd - mx)
            den = jnp.sum(pd, axis=-1, keepdims=True)
            po = None
            if qi > 0:
                po = jnp.exp2(so - mx)
                den = den + jnp.sum(po, axis=-1, keepdims=True)
            probs.append((pd, po, den))
        c1 = 1.0 / probs[0][2]
        r = lam * probs[0][2] / probs[1][2]
        o = jnp.dot((probs[0][0] - r * probs[1][0]).astype(BF16), va[lo:hi], preferred_element_type=F32)
        if qi > 0:
            o = o + jnp.dot((probs[0][1] - r * probs[1][1]).astype(BF16), va[:lo],
                            preferred_element_type=F32)
        o_ref[0, lo:hi, h * LANES:(h + 1) * LANES] = _subln(o * c1, subg_ref[...], lam_init).astype(o_ref.dtype)

    nq = seq // bq
    nxt = [scores(h, 0) for h in range(DA_HEADS_PER_STEP)]
    for qi in range(nq):
        cur = nxt
        if qi + 1 < nq:
            nxt = [scores(h, qi + 1) for h in range(DA_HEADS_PER_STEP)]
        for h in range(DA_HEADS_PER_STEP):
            attend(h, qi, cur[h])


def _da_prompt(q, kt_all, vb, layer, lams, subg, lam_init, name):
    b, s, _ = q.shape
    width = DA_HEADS_PER_STEP * LANES
    lam_spec = _const_spec((1, DA_DK))
    q_spec = pl.BlockSpec((1, s, width), lambda bi, h: (bi, 0, h))
    kt_spec = pl.BlockSpec((None, None, width, s), lambda bi, h: (layer, bi, h, 0))
    return pl.pallas_call(
        functools.partial(_da_prompt_body, lam_init=lam_init, bq=Q_TILE),
        grid=(b, DA_HEADS // DA_HEADS_PER_STEP),
        in_specs=[lam_spec, lam_spec, lam_spec, lam_spec, _const_spec((1, DA_DV)),
                  q_spec, kt_spec, q_spec],
        out_specs=q_spec,
        out_shape=jax.ShapeDtypeStruct((b, s, DA_WIDTH), BF16),
        compiler_params=_params(("parallel", "parallel")),
        name=name,
    )(*lams, subg, q, kt_all, vb)


def _da_sample_body(lq1, lk1, lq2, lk2, subg_ref, q_ref, ktp_ref, kn_ref, vp_ref, vn_ref, o_ref,
                    *, lam_init, past, sq):
    lam = _lambda(lq1, lk1, lq2, lk2, lam_init)
    q_chunk = (past + lax.broadcasted_iota(jnp.int32, (sq, 1), 0)) // CHUNK
    mask_p = (lax.broadcasted_iota(jnp.int32, (1, past), 1) // CHUNK) <= q_chunk
    mask_n = ((past + lax.broadcasted_iota(jnp.int32, (1, sq), 1)) // CHUNK) <= q_chunk
    for h in range(DA_HEADS):
        cols = slice(h * LANES, (h + 1) * LANES)
        qm = _split_maps(q_ref[0, :, cols])
        ktp = ktp_ref[cols, :].astype(BF16)
        kn = kn_ref[0, :, cols].astype(BF16)
        probs = []
        for m in range(2):
            sp = jnp.where(mask_p, jnp.dot(qm[m], ktp, preferred_element_type=F32), NEG)
            sn = jnp.where(mask_n, lax.dot_general(qm[m], kn, _NT, preferred_element_type=F32), NEG)
            mx = jnp.maximum(jnp.max(sp, axis=-1, keepdims=True), jnp.max(sn, axis=-1, keepdims=True))
            pp = jnp.exp2(sp - mx)
            pn = jnp.exp2(sn - mx)
            inv = 1.0 / (jnp.sum(pp, axis=-1, keepdims=True) + jnp.sum(pn, axis=-1, keepdims=True))
            probs.append((pp * inv, pn * inv))
        ap = (probs[0][0] - lam * probs[1][0]).astype(BF16)
        an = (probs[0][1] - lam * probs[1][1]).astype(BF16)
        vp = vp_ref[pl.ds(h, past, stride=DA_HEADS), :]
        o = (jnp.dot(ap, vp.astype(BF16), preferred_element_type=F32)
             + jnp.dot(an, vn_ref[0, :, cols].astype(BF16), preferred_element_type=F32))
        o_ref[0, :, cols] = _subln(o, subg_ref[...], lam_init).astype(o_ref.dtype)


def _da_sample(q, k_new, v_new, cache_kt, cache_v, layer, lams, subg, lam_init, name):
    b, sq, _ = q.shape
    past = cache_kt.shape[3]
    lam_spec = _const_spec((1, DA_DK))
    new_spec = pl.BlockSpec((1, sq, DA_QK_COLS), lambda bi: (bi, 0, 0))
    ktp_spec = pl.BlockSpec((None, None, DA_QK_COLS, past), lambda bi: (layer, bi, 0, 0))
    vp_spec = pl.BlockSpec((past * DA_HEADS, DA_DV), lambda bi: (layer * b + bi, 0))
    return pl.pallas_call(
        functools.partial(_da_sample_body, lam_init=lam_init, past=past, sq=sq),
        grid=(b,),
        in_specs=[lam_spec, lam_spec, lam_spec, lam_spec, _const_spec((1, DA_DV)),
                  new_spec, ktp_spec, new_spec, vp_spec, new_spec],
        out_specs=new_spec,
        out_shape=jax.ShapeDtypeStruct((b, sq, DA_WIDTH), BF16),
        compiler_params=_params(("parallel",)),
        name=name,
    )(*lams, subg, q, cache_kt, k_new, cache_v, v_new)


def _mem_kv_body(mem_ref, g_ref, wk_ref, wv_ref, kg_ref, k_in, v_in, k_ref, v_ref, kb_ref, vb_ref, *, nb):
    del k_in, v_in
    m = _rms(mem_ref[...], g_ref[...]).astype(BF16)
    k = jnp.dot(m, wk_ref[...], preferred_element_type=F32)
    v = jnp.dot(m, wv_ref[...], preferred_element_type=F32)
    vb_ref[...] = v.astype(vb_ref.dtype)
    for h in range(X_HEADS):
        cols = slice(h * X_DH, (h + 1) * X_DH)
        kn = _rms(k[:, cols], kg_ref[...])
        kb_ref[:, cols] = kn.astype(kb_ref.dtype)
        for bi in range(nb):
            rows = slice(bi * MEM_LEN, (bi + 1) * MEM_LEN)
            k_ref[bi, :, h, :] = kn[rows]
            v_ref[bi, :, h, :] = v[rows, cols]


def _mem_kv(mem, g, wk, wv, kg, mk_all, mv_all, layer, tm, name):
    t = mem.shape[0]
    nb = tm // MEM_LEN
    blk = pl.BlockSpec((tm, D_MODEL), lambda i: (i, 0))
    out_spec = pl.BlockSpec((None, nb, MEM_LEN, X_HEADS, X_DH), lambda i: (layer, i, 0, 0, 0))
    return pl.pallas_call(
        functools.partial(_mem_kv_body, nb=nb),
        grid=(t // tm,),
        in_specs=[blk, _const_spec((1, D_MODEL)), _const_spec((D_MODEL, D_MODEL)),
                  _const_spec((D_MODEL, D_MODEL)), _const_spec((1, X_DH)), _ANY, _ANY],
        out_specs=[out_spec, out_spec, blk, blk],
        out_shape=[jax.ShapeDtypeStruct(mk_all.shape, F32), jax.ShapeDtypeStruct(mv_all.shape, F32),
                   jax.ShapeDtypeStruct((t, D_MODEL), BF16), jax.ShapeDtypeStruct((t, D_MODEL), BF16)],
        input_output_aliases={5: 0, 6: 1},
        compiler_params=_params(("parallel",)),
        name=name,
    )(mem, g, wk, wv, kg, mk_all, mv_all)


def _mix_xattn_body(x_ref, o_ref, u_ref, gv_ref, ws_ref, bs_ref, wout_ref, g_ref, wq_ref, qg_ref,
                    mk_ref, mv_ref, wo_ref, xo_ref, cat_ref, att_ref, *, chunk):
    tm = x_ref.shape[0]
    sub = min(SUB_ROWS, tm)
    flat = len(mk_ref.shape) == 2
    tri = (lax.broadcasted_iota(jnp.int32, (chunk, chunk), 1)
           <= lax.broadcasted_iota(jnp.int32, (chunk, chunk), 0))
    ws = [jnp.where(tri, ws_ref[g], 0.0).astype(BF16) for g in range(GM_GROUPS)]

    def mix_out(j):
        r0 = j * sub
        rows = slice(r0, r0 + sub)
        cat_ref[rows, :DA_WIDTH] = o_ref[rows, :]
        for g in range(GM_GROUPS):
            cols = slice(g * GM_CH, (g + 1) * GM_CH)
            bias = bs_ref[g]
            for c in range(sub // chunk):
                cr = slice(r0 + c * chunk, r0 + (c + 1) * chunk)
                s = jnp.dot(ws[g], gv_ref[cr, cols].astype(BF16), preferred_element_type=F32) + bias
                cat_ref[cr, DA_WIDTH + g * GM_CH:DA_WIDTH + (g + 1) * GM_CH] = (
                    u_ref[cr, cols].astype(F32) * s).astype(BF16)
        x1 = x_ref[rows, :] + jnp.dot(cat_ref[rows, :], wout_ref[...], preferred_element_type=F32)
        h = _rms(x1, g_ref[...]).astype(BF16)
        return x1, jnp.dot(h, wq_ref[...], preferred_element_type=F32)

    def attend(j, x1, q):
        rows = slice(j * sub, (j + 1) * sub)
        for hh in range(X_HEADS):
            cols = slice(hh * X_DH, (hh + 1) * X_DH)
            mk = mk_ref[:, cols] if flat else mk_ref[:, hh, :]
            mv = mv_ref[:, cols] if flat else mv_ref[:, hh, :]
            qn = (_rms(q[:, cols], qg_ref[...]) * (X_DH ** -0.5)).astype(BF16)
            s = lax.dot_general(qn, mk.astype(BF16), _NT, preferred_element_type=F32)
            p = jnp.exp(s - jnp.max(s, axis=-1, keepdims=True))
            p = p * (1.0 / jnp.sum(p, axis=-1, keepdims=True))
            att_ref[rows, cols] = jnp.dot(p.astype(BF16), mv.astype(BF16),
                                          preferred_element_type=F32).astype(BF16)
        xo_ref[rows, :] = x1 + jnp.dot(att_ref[rows, :], wo_ref[...], preferred_element_type=F32)

    n_sub = tm // sub
    nxt = mix_out(0)
    for j in range(n_sub):
        cur = nxt
        if j + 1 < n_sub:
            nxt = mix_out(j + 1)
        attend(j, *cur)


def _mix_xattn(x, o, u, gv, ws, bs, wout, g, wq, qg, mk, mv, mem_spec, wo, tm, chunk, name):
    t = x.shape[0]
    row = lambda i: (i, 0)
    blk = pl.BlockSpec((tm, D_MODEL), row)
    half = pl.BlockSpec((tm, DA_WIDTH), row)
    sq = _const_spec((D_MODEL, D_MODEL))
    return pl.pallas_call(
        functools.partial(_mix_xattn_body, chunk=chunk),
        grid=(t // tm,),
        in_specs=[blk, half, half, half, _const_spec((GM_GROUPS, chunk, chunk)),
                  _const_spec((GM_GROUPS, chunk, GM_CH)), sq, _const_spec((1, D_MODEL)), sq,
                  _const_spec((1, X_DH)), mem_spec, mem_spec, sq],
        out_specs=blk,
        out_shape=jax.ShapeDtypeStruct((t, D_MODEL), F32),
        scratch_shapes=[pltpu.VMEM((tm, D_MODEL), BF16), pltpu.VMEM((tm, D_MODEL), BF16)],
        compiler_params=_params(("parallel",)),
        name=name,
    )(x, o, u, gv, ws, bs, wout, g, wq, qg, mk, mv, wo)


def _conv_ffn_body(x_ref, g_ref, wup_ref, cw_ref, cb_ref, wdn_ref, hist_ref, xo_ref, cst_ref,
                   gp_ref, carry_ref, act_ref, *, nb, sub, tiles_per_b):
    i = pl.program_id(0)
    tm = x_ref.shape[0]
    n_sub = tm // sub
    r = sub // nb
    chained = tiles_per_b > 1 or n_sub > 1
    first_tile = (i % tiles_per_b) == 0
    n_chunks = D_FF // FF_CHUNK

    def gate(j):
        h = _rms(x_ref[j * sub:(j + 1) * sub, :], g_ref[...]).astype(BF16)
        for c in range(n_chunks):
            cs = slice(c * FF_CHUNK, (c + 1) * FF_CHUNK)
            g = jnp.dot(h, wup_ref[:, cs], preferred_element_type=F32)
            up = jnp.dot(h, wup_ref[:, D_FF + c * FF_CHUNK:D_FF + (c + 1) * FF_CHUNK],
                         preferred_element_type=F32)
            for bi in range(nb):
                lrows = slice(bi * r, (bi + 1) * r)
                rows = slice(j * sub + bi * r, j * sub + (bi + 1) * r)
                if not chained:
                    prev = hist_ref[bi, :, cs]
                elif j == 0:
                    prev = jnp.where(first_tile, hist_ref[bi, :, cs], carry_ref[:, cs])
                else:
                    prev = carry_ref[:, cs]
                gp = gp_ref.at[c]
                gp[6:8, :] = prev
                gp[8:8 + r, :] = g[lrows]
                conv = (cb_ref[:, cs] + cw_ref[0:1, cs] * gp[6:6 + r, :]
                        + cw_ref[1:2, cs] * gp[7:7 + r, :] + cw_ref[2:3, cs] * g[lrows])
                act = conv * (1.0 / (1.0 + jnp.exp(-conv))) * up[lrows]
                act_ref[rows, cs] = act.astype(BF16)
                last2 = gp[6 + r:8 + r, :]
                cst_ref[bi, :, cs] = last2
                if chained:
                    carry_ref[:, cs] = last2

    def down(j):
        rows = slice(j * sub, (j + 1) * sub)
        xo_ref[rows, :] = x_ref[rows, :] + jnp.dot(act_ref[rows, :], wdn_ref[...],
                                                   preferred_element_type=F32)

    gate(0)
    for j in range(n_sub):
        if j + 1 < n_sub:
            gate(j + 1)
        down(j)


def _conv_ffn(x, g, wup, cw, cb, wdn, hist, tm, sub, nb, tiles_per_b, name):
    t = x.shape[0]
    row = lambda i: (i, 0)
    blk = pl.BlockSpec((tm, D_MODEL), row)
    hist_spec = pl.BlockSpec((nb, CONV_W - 1, D_FF), lambda i: (i // tiles_per_b, 0, 0))
    resident = lambda shape: pl.BlockSpec(shape, lambda i: (0, 0), pipeline_mode=pl.Buffered(1))
    return pl.pallas_call(
        functools.partial(_conv_ffn_body, nb=nb, sub=sub, tiles_per_b=tiles_per_b),
        grid=(t // tm,),
        in_specs=[blk, _const_spec((1, D_MODEL)), resident((D_MODEL, 2 * D_FF)),
                  _const_spec((CONV_W, D_FF)), _const_spec((1, D_FF)), resident((D_FF, D_MODEL)),
                  hist_spec],
        out_specs=[blk, hist_spec],
        out_shape=[jax.ShapeDtypeStruct((t, D_MODEL), F32),
                   jax.ShapeDtypeStruct(hist.shape, F32)],
        scratch_shapes=[pltpu.VMEM((D_FF // FF_CHUNK, sub // nb + 8, FF_CHUNK), F32),
                        pltpu.VMEM((CONV_W - 1, D_FF), F32),
                        pltpu.VMEM((tm, D_FF), BF16)],
        compiler_params=_params(("arbitrary",)),
        name=name,
    )(x, g, wup, cw, cb, wdn, hist)


def _rope_angles(pos):
    inv = ROPE_THETA ** (-jnp.arange(ROT_HALF, dtype=F32) / ROT_HALF)
    ang = pos.astype(F32)[:, None] * inv[None, :]
    return jnp.cos(ang), jnp.sin(ang)


def _rope_tables(pos):
    cos, sin = _rope_angles(pos)
    n = pos.shape[0]
    ones = jnp.ones((n, DA_DK - ROT_DIM), F32)
    zeros_h = jnp.zeros((n, ROT_HALF), F32)
    zeros_r = jnp.zeros((n, DA_DK - ROT_DIM), F32)
    cos64 = jnp.concatenate([cos, cos, ones], axis=1)
    sa64 = jnp.concatenate([-sin, zeros_h, zeros_r], axis=1)
    sb64 = jnp.concatenate([zeros_h, sin, zeros_r], axis=1)
    two = lambda a: jnp.concatenate([a, a], axis=1)
    return two(cos64), two(sa64), two(sb64)


def _post_attention(x, o, u, gv, w, mk, mv, mem_spec, hist, seq, tm, tag):
    t = x.shape[0]
    row2 = lambda a: a.reshape(1, -1)
    chunk = min(seq, GM_CHUNK)
    ws = w['gm_w_s'][:, :chunk, :chunk]
    bs = jnp.broadcast_to(w['gm_b'][:, :chunk, None], (GM_GROUPS, chunk, GM_CH))
    x = _mix_xattn(x, o.reshape(t, DA_WIDTH), u, gv, ws, bs, w['w_out'], row2(w['norm_x_g']), w['wq_c'],
                   row2(w['xq_norm_g']), mk, mv, mem_spec, w['wo_c'], min(tm, seq), chunk,
                   f"mix_xattn_{tag}")
    sub = min(SUB_ROWS, tm)
    return _conv_ffn(x, row2(w['norm_ffn_g']), w['w_up'], w['conv_w'], row2(w['conv_b']),
                     w['w_down'], hist, tm, sub, max(1, sub // seq), max(1, seq // tm), f"conv_ffn_{tag}")


def kernel(x_prompt, x_sample, cache_da_k, cache_da_v, cache_mem_k, cache_mem_v, state_ffn_conv, mem_prompt, norm_mix_g, w_in, da_q_norm_g, da_k_norm_g, lambda_q1, lambda_k1, lambda_q2, lambda_k2, da_subln_g, gm_norm_g, gm_w_s, gm_b, w_out, norm_x_g, norm_mem_g, wq_c, wk_c, wv_c, wo_c, xq_norm_g, xk_norm_g, norm_ffn_g, w_up, conv_w, conv_b, w_down):
    bp, sp, _ = x_prompt.shape
    bs_, ss, _ = x_sample.shape
    depth = w_in.shape[0]
    past = cache_da_k.shape[2]
    tm_w = min(WIDE_TILE, sp)
    tm_s = bs_ * ss
    tm_mem = min(TOKEN_TILE, bp * MEM_LEN)
    assert sp % tm_w == 0 and tm_w % SUB_ROWS == 0 and sp % Q_TILE == 0 and SUB_ROWS % GM_CHUNK == 0
    assert ss <= GM_CHUNK and tm_s % SUBLANES == 0 and ss >= CONV_W - 1 and ss % SUBLANES == 0
    assert (bp * MEM_LEN) % tm_mem == 0 and tm_mem % MEM_LEN == 0

    pos_p = jnp.arange(sp)
    rope_p = _rope_tables(pos_p)
    cos_p, sin_p = _rope_angles(pos_p)
    rope_kp = (cos_p.T, sin_p.T)
    rope_s = tuple(jnp.tile(a, (bs_, 1)) for a in _rope_tables(past + jnp.arange(ss)))
    cache_kt = jnp.transpose(cache_da_k, (0, 1, 3, 4, 5, 2)).reshape(depth, bs_, DA_QK_COLS, past)
    cache_v2 = cache_da_v.reshape(depth * bs_ * past * DA_HEADS, DA_DV)
    hist_p = jnp.zeros((bp, CONV_W - 1, D_FF), F32)
    mem_flat = mem_prompt.reshape(bp * MEM_LEN, D_MODEL)

    kt_all = lax.empty((depth, bp, DA_QK_COLS, sp), F32)
    v_all = lax.empty((depth * bp * sp * DA_HEADS, DA_DV), F32)
    mk_all = lax.empty((depth, bp, MEM_LEN, X_HEADS, X_DH), F32)
    mv_all = lax.empty((depth, bp, MEM_LEN, X_HEADS, X_DH), F32)

    xp = x_prompt.reshape(bp * sp, D_MODEL)
    xs = x_sample.reshape(tm_s, D_MODEL)
    outs = {n: [] for n in ('fc_p', 'dk_s', 'dv_s', 'gv_s', 'fc_s')}
    row2 = lambda a: a.reshape(1, -1)
    for l in range(depth):
        lam_init = 0.8 - 0.6 * math.exp(-0.3 * l)
        w = dict(gm_w_s=gm_w_s[l], gm_b=gm_b[l], w_out=w_out[l].astype(BF16),
                 norm_x_g=norm_x_g[l], wq_c=wq_c[l].astype(BF16), wo_c=wo_c[l].astype(BF16),
                 xq_norm_g=xq_norm_g[l], norm_ffn_g=norm_ffn_g[l], w_up=w_up[l].astype(BF16),
                 conv_w=conv_w[l], conv_b=conv_b[l], w_down=w_down[l].astype(BF16))
        w_in_l = w_in[l].astype(BF16)
        qg = jnp.tile(da_q_norm_g[l], 2).reshape(1, -1)
        kg = jnp.tile(da_k_norm_g[l], 2).reshape(1, -1)
        kg_col = jnp.broadcast_to(da_k_norm_g[l][:, None], (DA_DK, SUB_ROWS))
        gmg = jnp.tile(gm_norm_g[l], GM_GROUPS).reshape(1, -1)
        lams = [row2(a[l]) for a in (lambda_q1, lambda_k1, lambda_q2, lambda_k2)]
        subg = row2(da_subln_g[l])
        g_mix = row2(norm_mix_g[l])

        mk_all, mv_all, mkb, mvb = _mem_kv(mem_flat, row2(norm_mem_g[l]), wk_c[l].astype(BF16),
                                           wv_c[l].astype(BF16), row2(xk_norm_g[l]), mk_all, mv_all, l,
                                           tm_mem, f"mem_kv_{l}")
        q, kt_all, v_all, vb, u, gv = _mix_in_prompt(xp, g_mix, w_in_l, qg, kg_col, gmg, rope_p, rope_kp,
                                                     kt_all, v_all, l, sp, tm_w, f"mix_in_p{l}")
        o = _da_prompt(q.reshape(bp, sp, DA_QK_COLS), kt_all, vb.reshape(bp, sp, DA_WIDTH), l, lams, subg,
                       lam_init, f"da_p{l}")
        mem_spec_p = pl.BlockSpec((MEM_LEN, D_MODEL), lambda i: (i // (sp // tm_w), 0))
        xp, cst = _post_attention(xp, o, u, gv, w, mkb, mvb, mem_spec_p, hist_p, sp, tm_w, f"p{l}")
        outs['fc_p'].append(cst)

        q, k, v, u, gv = _mix_in_sample(xs, g_mix, w_in_l, qg, kg, gmg, rope_s, f"mix_in_s{l}")
        o = _da_sample(q.reshape(bs_, ss, DA_QK_COLS), k.reshape(bs_, ss, DA_QK_COLS),
                       v.reshape(bs_, ss, DA_WIDTH), cache_kt, cache_v2, l, lams, subg, lam_init,
                       f"da_s{l}")
        mem_spec_s = pl.BlockSpec((None, None, MEM_LEN, X_HEADS, X_DH), lambda i, l=l: (l, i, 0, 0, 0))
        xs, cst = _post_attention(xs, o, u, gv, w, cache_mem_k, cache_mem_v, mem_spec_s, state_ffn_conv[l],
                                  ss, tm_s, f"s{l}")
        outs['dk_s'].append(k.reshape(bs_, ss, DA_HEADS, 2, DA_DK))
        outs['dv_s'].append(v.reshape(bs_, ss, DA_HEADS, DA_DV))
        outs['gv_s'].append(gv.reshape(bs_, ss, GM_GROUPS, GM_CH))
        outs['fc_s'].append(cst)

    st = {n: jnp.stack(a) for n, a in outs.items()}
    dk_p = jnp.transpose(kt_all.reshape(depth, bp, DA_HEADS, 2, DA_DK, sp), (0, 1, 5, 2, 3, 4))
    dv_p = v_all.reshape(depth, bp, sp, DA_HEADS, DA_DV)
    return (xp.reshape(bp, sp, D_MODEL), xs.reshape(bs_, ss, D_MODEL), dk_p, dv_p, mk_all, mv_all,
            st['fc_p'], st['dk_s'], st['dv_s'], st['gv_s'], st['fc_s'])
```

```python
import functools
import math

import numpy as np
import jax
import jax.numpy as jnp
from jax import lax
from jax.experimental import pallas as pl
from jax.experimental.pallas import tpu as pltpu

F32 = jnp.float32
BF16 = jnp.bfloat16

D_MODEL = 1024
CHUNK = 64
DA_HEADS = 4
DA_DK = 64
DA_DV = 128
DA_QK_COLS = DA_HEADS * 2 * DA_DK
DA_WIDTH = DA_HEADS * DA_DV
ROT_DIM = 16
ROT_HALF = ROT_DIM // 2
ROPE_THETA = 500000.0
GM_GROUPS = 4
GM_CH = 128
GM_WIDTH = GM_GROUPS * GM_CH
GM_CHUNK = 128
MEM_LEN = 256
X_HEADS = 4
X_DH = 256
D_FF = 2816
CONV_W = 3
EPS = 1e-6
IN_COLS = 2 * DA_QK_COLS + DA_WIDTH + 2 * GM_WIDTH
NEG = float(np.finfo(np.float32).min)
Q_SCALE = (DA_DK ** -0.5) * math.log2(math.e)

LANES = 128
SUBLANES = 8
VMEM_LIMIT_BYTES = 56 * 1024 * 1024
FF_CHUNK = 256
TOKEN_TILE = 512
WIDE_TILE = 1024
Q_TILE = 256
DA_HEADS_PER_STEP = 2
SUB_ROWS = 512

_NT = (((1,), (1,)), ((), ()))


def _params(sem):
    return pltpu.CompilerParams(dimension_semantics=sem, vmem_limit_bytes=VMEM_LIMIT_BYTES)


def _const_spec(shape):
    n = len(shape)
    return pl.BlockSpec(shape, lambda *_: (0,) * n)


_ANY = pl.BlockSpec(memory_space=pl.ANY)


def _rms(x, g):
    return (x * lax.rsqrt(jnp.mean(x * x, axis=-1, keepdims=True) + EPS)) * g


def _gelu(x):
    return x * (0.5 * (1.0 + jnp.tanh(0.7978845608028654 * (x + 0.044715 * (x * x * x)))))


def _qk_slab(t, gain, cosv, sa, sb):
    lo = lax.broadcasted_iota(jnp.int32, (1, LANES), 1) < DA_DK
    sq = t * t
    s_lo = jnp.sum(jnp.where(lo, sq, 0.0), axis=-1, keepdims=True)
    s_hi = jnp.sum(jnp.where(lo, 0.0, sq), axis=-1, keepdims=True)
    ms = jnp.where(lo, s_lo, s_hi) * (1.0 / DA_DK)
    y = (t * lax.rsqrt(ms + EPS)) * gain
    return y * cosv + pltpu.roll(y, LANES - ROT_HALF, 1) * sa + pltpu.roll(y, ROT_HALF, 1) * sb


def _gmlp_cols(z, c0, gmg_ref, u_ref, gv_ref, rows=slice(None)):
    u_ref[rows, :] = _gelu(z[:, c0:c0 + GM_WIDTH]).astype(u_ref.dtype)
    c1 = c0 + GM_WIDTH
    for s in range(GM_GROUPS):
        cols = slice(s * GM_CH, (s + 1) * GM_CH)
        t = _gelu(z[:, c1 + s * GM_CH:c1 + (s + 1) * GM_CH])
        gv_ref[rows, cols] = _rms(t, gmg_ref[:, cols]).astype(gv_ref.dtype)


def _mix_in_sample_body(x_ref, g_ref, w_ref, qg_ref, kg_ref, gmg_ref, cos_ref, sa_ref, sb_ref,
                        q_ref, k_ref, v_ref, u_ref, gv_ref):
    h = _rms(x_ref[...], g_ref[...]).astype(BF16)
    z = jnp.dot(h, w_ref[...], preferred_element_type=F32)
    cosv, sa, sb = cos_ref[...], sa_ref[...], sb_ref[...]
    for s in range(DA_HEADS):
        cols = slice(s * LANES, (s + 1) * LANES)
        q_ref[:, cols] = (_qk_slab(z[:, cols], qg_ref[...], cosv, sa, sb) * Q_SCALE).astype(q_ref.dtype)
        kc = slice(DA_QK_COLS + s * LANES, DA_QK_COLS + (s + 1) * LANES)
        k_ref[:, cols] = _qk_slab(z[:, kc], kg_ref[...], cosv, sa, sb)
    c2 = 2 * DA_QK_COLS
    v_ref[...] = z[:, c2:c2 + DA_WIDTH]
    _gmlp_cols(z, c2 + DA_WIDTH, gmg_ref, u_ref, gv_ref)


def _mix_in_sample(x, g, w, qg, kg, gmg, rope, name):
    t = x.shape[0]
    row = pl.BlockSpec((t, DA_QK_COLS), lambda i: (0, 0))
    rope_spec = _const_spec((t, LANES))
    return pl.pallas_call(
        _mix_in_sample_body,
        grid=(1,),
        in_specs=[_const_spec((t, D_MODEL)), _const_spec((1, D_MODEL)), _const_spec((D_MODEL, IN_COLS)),
                  _const_spec((1, LANES)), _const_spec((1, LANES)), _const_spec((1, GM_WIDTH)),
                  rope_spec, rope_spec, rope_spec],
        out_specs=[row] * 5,
        out_shape=[jax.ShapeDtypeStruct((t, DA_QK_COLS), BF16),
                   jax.ShapeDtypeStruct((t, DA_QK_COLS), F32),
                   jax.ShapeDtypeStruct((t, DA_WIDTH), F32),
                   jax.ShapeDtypeStruct((t, GM_WIDTH), BF16),
                   jax.ShapeDtypeStruct((t, GM_WIDTH), F32)],
        compiler_params=_params(("arbitrary",)),
        name=name,
    )(x, g, w, qg, kg, gmg, *rope)


def _mix_in_prompt_body(x_ref, g_ref, w_ref, qg_ref, kg_ref, gmg_ref, cos_ref, sa_ref, sb_ref,
                        cosk_ref, sink_ref, kt_in, v_in, q_ref, kt_ref, v_ref, vb_ref, u_ref, gv_ref):
    del kt_in, v_in
    tm = x_ref.shape[0]
    c_k, c_v = DA_QK_COLS, 2 * DA_QK_COLS

    def project(j):
        rows = slice(j * SUB_ROWS, (j + 1) * SUB_ROWS)
        h = _rms(x_ref[rows, :], g_ref[...]).astype(BF16)
        return jnp.dot(h, w_ref[...], preferred_element_type=F32)

    def finish(j, z):
        rows = slice(j * SUB_ROWS, (j + 1) * SUB_ROWS)
        cosv, sa, sb = cos_ref[rows, :], sa_ref[rows, :], sb_ref[rows, :]
        for s in range(DA_HEADS):
            cols = slice(s * LANES, (s + 1) * LANES)
            q_ref[rows, cols] = (_qk_slab(z[:, cols], qg_ref[...], cosv, sa, sb) * Q_SCALE).astype(q_ref.dtype)
            v_ref[pl.ds(j * SUB_ROWS * DA_HEADS + s, SUB_ROWS, stride=DA_HEADS), :] = (
                z[:, c_v + s * DA_DV:c_v + (s + 1) * DA_DV])
        vb_ref[rows, :] = z[:, c_v:c_v + DA_WIDTH].astype(vb_ref.dtype)
        zk = z[:, c_k:c_k + DA_QK_COLS].T
        ck, sk, kg = cosk_ref[:, rows], sink_ref[:, rows], kg_ref[...]
        for grp in range(2 * DA_HEADS):
            r0 = grp * DA_DK
            t = zk[r0:r0 + DA_DK, :]
            inv = lax.rsqrt(jnp.sum(t * t, axis=0, keepdims=True) * (1.0 / DA_DK) + EPS)
            y = (t * inv) * kg
            ya, yb = y[:ROT_HALF], y[ROT_HALF:ROT_DIM]
            kt_ref[r0:r0 + ROT_HALF, rows] = ya * ck - yb * sk
            kt_ref[r0 + ROT_HALF:r0 + ROT_DIM, rows] = yb * ck + ya * sk
            kt_ref[r0 + ROT_DIM:r0 + DA_DK, rows] = y[ROT_DIM:]
        _gmlp_cols(z, c_v + DA_WIDTH, gmg_ref, u_ref, gv_ref, rows)

    n_sub = tm // SUB_ROWS
    nxt = project(0)
    for j in range(n_sub):
        cur = nxt
        if j + 1 < n_sub:
            nxt = project(j + 1)
        finish(j, cur)


def _mix_in_prompt(x, g, w, qg, kg_col, gmg, rope, rope_k, kt_all, v_all, layer, seq, tm, name):
    t = x.shape[0]
    per_b = seq // tm
    row = lambda i: (i, 0)
    rope_spec = pl.BlockSpec((tm, LANES), lambda i: (i % per_b, 0))
    ropek_spec = pl.BlockSpec((SUBLANES, tm), lambda i: (0, i % per_b))
    half = pl.BlockSpec((tm, DA_QK_COLS), row)
    kt_spec = pl.BlockSpec((None, None, DA_QK_COLS, tm), lambda i: (layer, i // per_b, 0, i % per_b))
    v_spec = pl.BlockSpec((tm * DA_HEADS, DA_DV), lambda i: (layer * (t // tm) + i, 0))
    return pl.pallas_call(
        _mix_in_prompt_body,
        grid=(t // tm,),
        in_specs=[pl.BlockSpec((tm, D_MODEL), row), _const_spec((1, D_MODEL)),
                  _const_spec((D_MODEL, IN_COLS)),
                  _const_spec((1, LANES)), _const_spec((DA_DK, SUB_ROWS)), _const_spec((1, GM_WIDTH)),
                  rope_spec, rope_spec, rope_spec, ropek_spec, ropek_spec, _ANY, _ANY],
        out_specs=[half, kt_spec, v_spec, half, half, half],
        out_shape=[jax.ShapeDtypeStruct((t, DA_QK_COLS), BF16),
                   jax.ShapeDtypeStruct(kt_all.shape, F32),
                   jax.ShapeDtypeStruct(v_all.shape, F32),
                   jax.ShapeDtypeStruct((t, DA_WIDTH), BF16),
                   jax.ShapeDtypeStruct((t, GM_WIDTH), BF16),
                   jax.ShapeDtypeStruct((t, GM_WIDTH), BF16)],
        input_output_aliases={11: 1, 12: 2},
        compiler_params=_params(("parallel",)),
        name=name,
    )(x, g, w, qg, kg_col, gmg, *rope, *rope_k, kt_all, v_all)


def _lambda(lq1, lk1, lq2, lk2, lam_init):
    a = jnp.sum(lq1[...] * lk1[...], axis=-1, keepdims=True)
    b = jnp.sum(lq2[...] * lk2[...], axis=-1, keepdims=True)
    return jnp.exp(a) - jnp.exp(b) + lam_init


def _subln(o, g, lam_init):
    return _rms(o, g) * (1.0 - lam_init)


def _split_maps(q):
    lo = lax.broadcasted_iota(jnp.int32, (1, LANES), 1) < DA_DK
    zero = jnp.zeros_like(q)
    return jnp.where(lo, q, zero), jnp.where(lo, zero, q)


def _da_prompt_body(lq1, lk1, lq2, lk2, subg_ref, q_ref, kt_ref, v_ref, o_ref, *, lam_init, bq):
    seq = q_ref.shape[1]
    lam = _lambda(lq1, lk1, lq2, lk2, lam_init)
    mask = ((lax.broadcasted_iota(jnp.int32, (bq, bq), 1) // CHUNK)
            <= (lax.broadcasted_iota(jnp.int32, (bq, bq), 0) // CHUNK))
    heads = []
    for h in range(DA_HEADS_PER_STEP):
        cols = slice(h * LANES, (h + 1) * LANES)
        heads.append((_split_maps(q_ref[0, :, cols]),
                      kt_ref[cols, :].astype(BF16),
                      jnp.concatenate([v_ref[0, :, cols], jnp.ones((seq, DA_DV), BF16)], axis=1)))

    def scores(h, qi):
        qm, kt, _ = heads[h]
        lo, hi = qi * bq, (qi + 1) * bq
        out = []
        for m in range(2):
            q = qm[m][lo:hi]
            sd = jnp.dot(q, kt[:, lo:hi], preferred_element_type=F32)
            so = jnp.dot(q, kt[:, :lo], preferred_element_type=F32) if qi > 0 else None
            out.append((sd, so))
        return out

    def attend(h, qi, cur):
        va = heads[h][2]
        lo, hi = qi * bq, (qi + 1) * bq
        outs = []
        for m in range(2):
            sd, so = cur[m]
            sd = jnp.where(mask, sd, NEG)
            mx = jnp.max(sd, axis=-1, keepdims=True)
            if qi > 0:
                mx = jnp.maximum(mx, jnp.max(so, axis=-1, keepdims=True))
            ov = jnp.dot(jnp.exp2(sd - mx).astype(BF16), va[lo:hi], preferred_element_type=F32)
            if qi > 0:
                ov = ov + jnp.dot(jnp.exp2(so - mx).astype(BF16), va[:lo], preferred_element_type=F32)
            outs.append(ov[:, :DA_DV] * (1.0 / ov[:, DA_DV:DA_DV + 1]))
        o = outs[0] - lam * outs[1]
        o_ref[0, lo:hi, h * LANES:(h + 1) * LANES] = _subln(o, subg_ref[...], lam_init).astype(o_ref.dtype)

    nq = seq // bq
    nxt = [scores(h, 0) for h in range(DA_HEADS_PER_STEP)]
    for qi in range(nq):
        cur = nxt
        if qi + 1 < nq:
            nxt = [scores(h, qi + 1) for h in range(DA_HEADS_PER_STEP)]
        for h in range(DA_HEADS_PER_STEP):
            attend(h, qi, cur[h])


def _da_prompt(q, kt_all, vb, layer, lams, subg, lam_init, name):
    b, s, _ = q.shape
    width = DA_HEADS_PER_STEP * LANES
    lam_spec = _const_spec((1, DA_DK))
    q_spec = pl.BlockSpec((1, s, width), lambda bi, h: (bi, 0, h))
    kt_spec = pl.BlockSpec((None, None, width, s), lambda bi, h: (layer, bi, h, 0))
    return pl.pallas_call(
        functools.partial(_da_prompt_body, lam_init=lam_init, bq=Q_TILE),
        grid=(b, DA_HEADS // DA_HEADS_PER_STEP),
        in_specs=[lam_spec, lam_spec, lam_spec, lam_spec, _const_spec((1, DA_DV)),
                  q_spec, kt_spec, q_spec],
        out_specs=q_spec,
        out_shape=jax.ShapeDtypeStruct((b, s, DA_WIDTH), BF16),
        compiler_params=_params(("parallel", "parallel")),
        name=name,
    )(*lams, subg, q, kt_all, vb)


def _da_sample_body(lq1, lk1, lq2, lk2, subg_ref, q_ref, ktp_ref, kn_ref, vp_ref, vn_ref, o_ref,
                    *, lam_init, past, sq):
    lam = _lambda(lq1, lk1, lq2, lk2, lam_init)
    q_chunk = (past + lax.broadcasted_iota(jnp.int32, (sq, 1), 0)) // CHUNK
    mask_p = (lax.broadcasted_iota(jnp.int32, (1, past), 1) // CHUNK) <= q_chunk
    mask_n = ((past + lax.broadcasted_iota(jnp.int32, (1, sq), 1)) // CHUNK) <= q_chunk
    for h in range(DA_HEADS):
        cols = slice(h * LANES, (h + 1) * LANES)
        qm = _split_maps(q_ref[0, :, cols])
        ktp = ktp_ref[cols, :].astype(BF16)
        kn = kn_ref[0, :, cols].astype(BF16)
        probs = []
        for m in range(2):
            sp = jnp.where(mask_p, jnp.dot(qm[m], ktp, preferred_element_type=F32), NEG)
            sn = jnp.where(mask_n, lax.dot_general(qm[m], kn, _NT, preferred_element_type=F32), NEG)
            mx = jnp.maximum(jnp.max(sp, axis=-1, keepdims=True), jnp.max(sn, axis=-1, keepdims=True))
            pp = jnp.exp2(sp - mx)
            pn = jnp.exp2(sn - mx)
            inv = 1.0 / (jnp.sum(pp, axis=-1, keepdims=True) + jnp.sum(pn, axis=-1, keepdims=True))
            probs.append((pp * inv, pn * inv))
        ap = (probs[0][0] - lam * probs[1][0]).astype(BF16)
        an = (probs[0][1] - lam * probs[1][1]).astype(BF16)
        vp = vp_ref[pl.ds(h, past, stride=DA_HEADS), :]
        o = (jnp.dot(ap, vp.astype(BF16), preferred_element_type=F32)
             + jnp.dot(an, vn_ref[0, :, cols].astype(BF16), preferred_element_type=F32))
        o_ref[0, :, cols] = _subln(o, subg_ref[...], lam_init).astype(o_ref.dtype)


def _da_sample(q, k_new, v_new, cache_kt, cache_v, layer, lams, subg, lam_init, name):
    b, sq, _ = q.shape
    past = cache_kt.shape[3]
    lam_spec = _const_spec((1, DA_DK))
    new_spec = pl.BlockSpec((1, sq, DA_QK_COLS), lambda bi: (bi, 0, 0))
    ktp_spec = pl.BlockSpec((None, None, DA_QK_COLS, past), lambda bi: (layer, bi, 0, 0))
    vp_spec = pl.BlockSpec((past * DA_HEADS, DA_DV), lambda bi: (layer * b + bi, 0))
    return pl.pallas_call(
        functools.partial(_da_sample_body, lam_init=lam_init, past=past, sq=sq),
        grid=(b,),
        in_specs=[lam_spec, lam_spec, lam_spec, lam_spec, _const_spec((1, DA_DV)),
                  new_spec, ktp_spec, new_spec, vp_spec, new_spec],
        out_specs=new_spec,
        out_shape=jax.ShapeDtypeStruct((b, sq, DA_WIDTH), BF16),
        compiler_params=_params(("parallel",)),
        name=name,
    )(*lams, subg, q, cache_kt, k_new, cache_v, v_new)


def _mem_kv_body(mem_ref, g_ref, wk_ref, wv_ref, kg_ref, k_in, v_in, k_ref, v_ref, kb_ref, vb_ref, *, nb):
    del k_in, v_in
    m = _rms(mem_ref[...], g_ref[...]).astype(BF16)
    k = jnp.dot(m, wk_ref[...], preferred_element_type=F32)
    v = jnp.dot(m, wv_ref[...], preferred_element_type=F32)
    vb_ref[...] = v.astype(vb_ref.dtype)
    for h in range(X_HEADS):
        cols = slice(h * X_DH, (h + 1) * X_DH)
        kn = _rms(k[:, cols], kg_ref[...])
        kb_ref[:, cols] = kn.astype(kb_ref.dtype)
        for bi in range(nb):
            rows = slice(bi * MEM_LEN, (bi + 1) * MEM_LEN)
            k_ref[bi, :, h, :] = kn[rows]
            v_ref[bi, :, h, :] = v[rows, cols]


def _mem_kv(mem, g, wk, wv, kg, mk_all, mv_all, layer, tm, name):
    t = mem.shape[0]
    nb = tm // MEM_LEN
    blk = pl.BlockSpec((tm, D_MODEL), lambda i: (i, 0))
    out_spec = pl.BlockSpec((None, nb, MEM_LEN, X_HEADS, X_DH), lambda i: (layer, i, 0, 0, 0))
    return pl.pallas_call(
        functools.partial(_mem_kv_body, nb=nb),
        grid=(t // tm,),
        in_specs=[blk, _const_spec((1, D_MODEL)), _const_spec((D_MODEL, D_MODEL)),
                  _const_spec((D_MODEL, D_MODEL)), _const_spec((1, X_DH)), _ANY, _ANY],
        out_specs=[out_spec, out_spec, blk, blk],
        out_shape=[jax.ShapeDtypeStruct(mk_all.shape, F32), jax.ShapeDtypeStruct(mv_all.shape, F32),
                   jax.ShapeDtypeStruct((t, D_MODEL), BF16), jax.ShapeDtypeStruct((t, D_MODEL), BF16)],
        input_output_aliases={5: 0, 6: 1},
        compiler_params=_params(("parallel",)),
        name=name,
    )(mem, g, wk, wv, kg, mk_all, mv_all)


def _mix_xattn_body(x_ref, o_ref, u_ref, gv_ref, ws_ref, bs_ref, wout_ref, g_ref, wq_ref, qg_ref,
                    mk_ref, mv_ref, wo_ref, xo_ref, cat_ref, att_ref, *, chunk):
    tm = x_ref.shape[0]
    sub = min(SUB_ROWS, tm)
    flat = len(mk_ref.shape) == 2
    tri = (lax.broadcasted_iota(jnp.int32, (chunk, chunk), 1)
           <= lax.broadcasted_iota(jnp.int32, (chunk, chunk), 0))
    ws = [jnp.where(tri, ws_ref[g], 0.0).astype(BF16) for g in range(GM_GROUPS)]

    def mix_out(j):
        r0 = j * sub
        rows = slice(r0, r0 + sub)
        cat_ref[rows, :DA_WIDTH] = o_ref[rows, :]
        for g in range(GM_GROUPS):
            cols = slice(g * GM_CH, (g + 1) * GM_CH)
            bias = bs_ref[g]
            for c in range(sub // chunk):
                cr = slice(r0 + c * chunk, r0 + (c + 1) * chunk)
                s = jnp.dot(ws[g], gv_ref[cr, cols].astype(BF16), preferred_element_type=F32) + bias
                cat_ref[cr, DA_WIDTH + g * GM_CH:DA_WIDTH + (g + 1) * GM_CH] = (
                    u_ref[cr, cols].astype(F32) * s).astype(BF16)
        x1 = x_ref[rows, :] + jnp.dot(cat_ref[rows, :], wout_ref[...], preferred_element_type=F32)
        h = _rms(x1, g_ref[...]).astype(BF16)
        return x1, jnp.dot(h, wq_ref[...], preferred_element_type=F32)

    def attend(j, x1, q):
        rows = slice(j * sub, (j + 1) * sub)
        for hh in range(X_HEADS):
            cols = slice(hh * X_DH, (hh + 1) * X_DH)
            mk = mk_ref[:, cols] if flat else mk_ref[:, hh, :]
            mv = mv_ref[:, cols] if flat else mv_ref[:, hh, :]
            qn = (_rms(q[:, cols], qg_ref[...]) * (X_DH ** -0.5)).astype(BF16)
            s = lax.dot_general(qn, mk.astype(BF16), _NT, preferred_element_type=F32)
            p = jnp.exp(s - jnp.max(s, axis=-1, keepdims=True))
            p = p * (1.0 / jnp.sum(p, axis=-1, keepdims=True))
            att_ref[rows, cols] = jnp.dot(p.astype(BF16), mv.astype(BF16),
                                          preferred_element_type=F32).astype(BF16)
        xo_ref[rows, :] = x1 + jnp.dot(att_ref[rows, :], wo_ref[...], preferred_element_type=F32)

    n_sub = tm // sub
    nxt = mix_out(0)
    for j in range(n_sub):
        cur = nxt
        if j + 1 < n_sub:
            nxt = mix_out(j + 1)
        attend(j, *cur)


def _mix_xattn(x, o, u, gv, ws, bs, wout, g, wq, qg, mk, mv, mem_spec, wo, tm, chunk, name):
    t = x.shape[0]
    row = lambda i: (i, 0)
    blk = pl.BlockSpec((tm, D_MODEL), row)
    half = pl.BlockSpec((tm, DA_WIDTH), row)
    sq = _const_spec((D_MODEL, D_MODEL))
    return pl.pallas_call(
        functools.partial(_mix_xattn_body, chunk=chunk),
        grid=(t // tm,),
        in_specs=[blk, half, half, half, _const_spec((GM_GROUPS, chunk, chunk)),
                  _const_spec((GM_GROUPS, chunk, GM_CH)), sq, _const_spec((1, D_MODEL)), sq,
                  _const_spec((1, X_DH)), mem_spec, mem_spec, sq],
        out_specs=blk,
        out_shape=jax.ShapeDtypeStruct((t, D_MODEL), F32),
        scratch_shapes=[pltpu.VMEM((tm, D_MODEL), BF16), pltpu.VMEM((tm, D_MODEL), BF16)],
        compiler_params=_params(("parallel",)),
        name=name,
    )(x, o, u, gv, ws, bs, wout, g, wq, qg, mk, mv, wo)


def _conv_ffn_body(x_ref, g_ref, wup_ref, cw_ref, cb_ref, wdn_ref, hist_ref, xo_ref, cst_ref,
                   gp_ref, carry_ref, act_ref, *, nb, sub, tiles_per_b):
    i = pl.program_id(0)
    tm = x_ref.shape[0]
    n_sub = tm // sub
    r = sub // nb
    chained = tiles_per_b > 1 or n_sub > 1
    first_tile = (i % tiles_per_b) == 0
    n_chunks = D_FF // FF_CHUNK

    def gate(j):
        h = _rms(x_ref[j * sub:(j + 1) * sub, :], g_ref[...]).astype(BF16)
        for c in range(n_chunks):
            cs = slice(c * FF_CHUNK, (c + 1) * FF_CHUNK)
            g = jnp.dot(h, wup_ref[:, cs], preferred_element_type=F32)
            up = jnp.dot(h, wup_ref[:, D_FF + c * FF_CHUNK:D_FF + (c + 1) * FF_CHUNK],
                         preferred_element_type=F32)
            for bi in range(nb):
                lrows = slice(bi * r, (bi + 1) * r)
                rows = slice(j * sub + bi * r, j * sub + (bi + 1) * r)
                if not chained:
                    prev = hist_ref[bi, :, cs]
                elif j == 0:
                    prev = jnp.where(first_tile, hist_ref[bi, :, cs], carry_ref[:, cs])
                else:
                    prev = carry_ref[:, cs]
                gp = gp_ref.at[c]
                gp[6:8, :] = prev
                gp[8:8 + r, :] = g[lrows]
                conv = (cb_ref[:, cs] + cw_ref[0:1, cs] * gp[6:6 + r, :]
                        + cw_ref[1:2, cs] * gp[7:7 + r, :] + cw_ref[2:3, cs] * g[lrows])
                act = conv * (1.0 / (1.0 + jnp.exp(-conv))) * up[lrows]
                act_ref[rows, cs] = act.astype(BF16)
                last2 = gp[6 + r:8 + r, :]
                cst_ref[bi, :, cs] = last2
                if chained:
                    carry_ref[:, cs] = last2

    def down(j):
        rows = slice(j * sub, (j + 1) * sub)
        xo_ref[rows, :] = x_ref[rows, :] + jnp.dot(act_ref[rows, :], wdn_ref[...],
                                                   preferred_element_type=F32)

    gate(0)
    for j in range(n_sub):
        if j + 1 < n_sub:
            gate(j + 1)
        down(j)


def _conv_ffn(x, g, wup, cw, cb, wdn, hist, tm, sub, nb, tiles_per_b, name):
    t = x.shape[0]
    row = lambda i: (i, 0)
    blk = pl.BlockSpec((tm, D_MODEL), row)
    hist_spec = pl.BlockSpec((nb, CONV_W - 1, D_FF), lambda i: (i // tiles_per_b, 0, 0))
    resident = lambda shape: pl.BlockSpec(shape, lambda i: (0, 0), pipeline_mode=pl.Buffered(1))
    return pl.pallas_call(
        functools.partial(_conv_ffn_body, nb=nb, sub=sub, tiles_per_b=tiles_per_b),
        grid=(t // tm,),
        in_specs=[blk, _const_spec((1, D_MODEL)), resident((D_MODEL, 2 * D_FF)),
                  _const_spec((CONV_W, D_FF)), _const_spec((1, D_FF)), resident((D_FF, D_MODEL)),
                  hist_spec],
        out_specs=[blk, hist_spec],
        out_shape=[jax.ShapeDtypeStruct((t, D_MODEL), F32),
                   jax.ShapeDtypeStruct(hist.shape, F32)],
        scratch_shapes=[pltpu.VMEM((D_FF // FF_CHUNK, sub // nb + 8, FF_CHUNK), F32),
                        pltpu.VMEM((CONV_W - 1, D_FF), F32),
                        pltpu.VMEM((tm, D_FF), BF16)],
        compiler_params=_params(("arbitrary",)),
        name=name,
    )(x, g, wup, cw, cb, wdn, hist)


def _rope_angles(pos):
    inv = ROPE_THETA ** (-jnp.arange(ROT_HALF, dtype=F32) / ROT_HALF)
    ang = pos.astype(F32)[:, None] * inv[None, :]
    return jnp.cos(ang), jnp.sin(ang)


def _rope_tables(pos):
    cos, sin = _rope_angles(pos)
    n = pos.shape[0]
    ones = jnp.ones((n, DA_DK - ROT_DIM), F32)
    zeros_h = jnp.zeros((n, ROT_HALF), F32)
    zeros_r = jnp.zeros((n, DA_DK - ROT_DIM), F32)
    cos64 = jnp.concatenate([cos, cos, ones], axis=1)
    sa64 = jnp.concatenate([-sin, zeros_h, zeros_r], axis=1)
    sb64 = jnp.concatenate([zeros_h, sin, zeros_r], axis=1)
    two = lambda a: jnp.concatenate([a, a], axis=1)
    return two(cos64), two(sa64), two(sb64)


def _post_attention(x, o, u, gv, w, mk, mv, mem_spec, hist, seq, tm, tag):
    t = x.shape[0]
    row2 = lambda a: a.reshape(1, -1)
    chunk = min(seq, GM_CHUNK)
    ws = w['gm_w_s'][:, :chunk, :chunk]
    bs = jnp.broadcast_to(w['gm_b'][:, :chunk, None], (GM_GROUPS, chunk, GM_CH))
    x = _mix_xattn(x, o.reshape(t, DA_WIDTH), u, gv, ws, bs, w['w_out'], row2(w['norm_x_g']), w['wq_c'],
                   row2(w['xq_norm_g']), mk, mv, mem_spec, w['wo_c'], min(tm, seq), chunk,
                   f"mix_xattn_{tag}")
    sub = min(SUB_ROWS, tm)
    return _conv_ffn(x, row2(w['norm_ffn_g']), w['w_up'], w['conv_w'], row2(w['conv_b']),
                     w['w_down'], hist, tm, sub, max(1, sub // seq), max(1, seq // tm), f"conv_ffn_{tag}")


def kernel(x_prompt, x_sample, cache_da_k, cache_da_v, cache_mem_k, cache_mem_v, state_ffn_conv, mem_prompt, norm_mix_g, w_in, da_q_norm_g, da_k_norm_g, lambda_q1, lambda_k1, lambda_q2, lambda_k2, da_subln_g, gm_norm_g, gm_w_s, gm_b, w_out, norm_x_g, norm_mem_g, wq_c, wk_c, wv_c, wo_c, xq_norm_g, xk_norm_g, norm_ffn_g, w_up, conv_w, conv_b, w_down):
    bp, sp, _ = x_prompt.shape
    bs_, ss, _ = x_sample.shape
    depth = w_in.shape[0]
    past = cache_da_k.shape[2]
    tm_w = min(WIDE_TILE, sp)
    tm_s = bs_ * ss
    tm_mem = min(TOKEN_TILE, bp * MEM_LEN)
    assert sp % tm_w == 0 and tm_w % SUB_ROWS == 0 and sp % Q_TILE == 0 and SUB_ROWS % GM_CHUNK == 0
    assert ss <= GM_CHUNK and tm_s % SUBLANES == 0 and ss >= CONV_W - 1 and ss % SUBLANES == 0
    assert (bp * MEM_LEN) % tm_mem == 0 and tm_mem % MEM_LEN == 0

    pos_p = jnp.arange(sp)
    rope_p = _rope_tables(pos_p)
    cos_p, sin_p = _rope_angles(pos_p)
    rope_kp = (cos_p.T, sin_p.T)
    rope_s = tuple(jnp.tile(a, (bs_, 1)) for a in _rope_tables(past + jnp.arange(ss)))
    cache_kt = jnp.transpose(cache_da_k, (0, 1, 3, 4, 5, 2)).reshape(depth, bs_, DA_QK_COLS, past)
    cache_v2 = cache_da_v.reshape(depth * bs_ * past * DA_HEADS, DA_DV)
    hist_p = jnp.zeros((bp, CONV_W - 1, D_FF), F32)
    mem_flat = mem_prompt.reshape(bp * MEM_LEN, D_MODEL)

    kt_all = lax.empty((depth, bp, DA_QK_COLS, sp), F32)
    v_all = lax.empty((depth * bp * sp * DA_HEADS, DA_DV), F32)
    mk_all = lax.empty((depth, bp, MEM_LEN, X_HEADS, X_DH), F32)
    mv_all = lax.empty((depth, bp, MEM_LEN, X_HEADS, X_DH), F32)

    xp = x_prompt.reshape(bp * sp, D_MODEL)
    xs = x_sample.reshape(tm_s, D_MODEL)
    outs = {n: [] for n in ('fc_p', 'dk_s', 'dv_s', 'gv_s', 'fc_s')}
    row2 = lambda a: a.reshape(1, -1)
    for l in range(depth):
        lam_init = 0.8 - 0.6 * math.exp(-0.3 * l)
        w = dict(gm_w_s=gm_w_s[l], gm_b=gm_b[l], w_out=w_out[l].astype(BF16),
                 norm_x_g=norm_x_g[l], wq_c=wq_c[l].astype(BF16), wo_c=wo_c[l].astype(BF16),
                 xq_norm_g=xq_norm_g[l], norm_ffn_g=norm_ffn_g[l], w_up=w_up[l].astype(BF16),
                 conv_w=conv_w[l], conv_b=conv_b[l], w_down=w_down[l].astype(BF16))
        w_in_l = w_in[l].astype(BF16)
        qg = jnp.tile(da_q_norm_g[l], 2).reshape(1, -1)
        kg = jnp.tile(da_k_norm_g[l], 2).reshape(1, -1)
        kg_col = jnp.broadcast_to(da_k_norm_g[l][:, None], (DA_DK, SUB_ROWS))
        gmg = jnp.tile(gm_norm_g[l], GM_GROUPS).reshape(1, -1)
        lams = [row2(a[l]) for a in (lambda_q1, lambda_k1, lambda_q2, lambda_k2)]
        subg = row2(da_subln_g[l])
        g_mix = row2(norm_mix_g[l])

        mk_all, mv_all, mkb, mvb = _mem_kv(mem_flat, row2(norm_mem_g[l]), wk_c[l].astype(BF16),
                                           wv_c[l].astype(BF16), row2(xk_norm_g[l]), mk_all, mv_all, l,
                                           tm_mem, f"mem_kv_{l}")
        q, kt_all, v_all, vb, u, gv = _mix_in_prompt(xp, g_mix, w_in_l, qg, kg_col, gmg, rope_p, rope_kp,
                                                     kt_all, v_all, l, sp, tm_w, f"mix_in_p{l}")
        o = _da_prompt(q.reshape(bp, sp, DA_QK_COLS), kt_all, vb.reshape(bp, sp, DA_WIDTH), l, lams, subg,
                       lam_init, f"da_p{l}")
        mem_spec_p = pl.BlockSpec((MEM_LEN, D_MODEL), lambda i: (i // (sp // tm_w), 0))
        xp, cst = _post_attention(xp, o, u, gv, w, mkb, mvb, mem_spec_p, hist_p, sp, tm_w, f"p{l}")
        outs['fc_p'].append(cst)

        q, k, v, u, gv = _mix_in_sample(xs, g_mix, w_in_l, qg, kg, gmg, rope_s, f"mix_in_s{l}")
        o = _da_sample(q.reshape(bs_, ss, DA_QK_COLS), k.reshape(bs_, ss, DA_QK_COLS),
                       v.reshape(bs_, ss, DA_WIDTH), cache_kt, cache_v2, l, lams, subg, lam_init,
                       f"da_s{l}")
        mem_spec_s = pl.BlockSpec((None, None, MEM_LEN, X_HEADS, X_DH), lambda i, l=l: (l, i, 0, 0, 0))
        xs, cst = _post_attention(xs, o, u, gv, w, cache_mem_k, cache_mem_v, mem_spec_s, state_ffn_conv[l],
                                  ss, tm_s, f"s{l}")
        outs['dk_s'].append(k.reshape(bs_, ss, DA_HEADS, 2, DA_DK))
        outs['dv_s'].append(v.reshape(bs_, ss, DA_HEADS, DA_DV))
        outs['gv_s'].append(gv.reshape(bs_, ss, GM_GROUPS, GM_CH))
        outs['fc_s'].append(cst)

    st = {n: jnp.stack(a) for n, a in outs.items()}
    dk_p = jnp.transpose(kt_all.reshape(depth, bp, DA_HEADS, 2, DA_DK, sp), (0, 1, 5, 2, 3, 4))
    dv_p = v_all.reshape(depth, bp, sp, DA_HEADS, DA_DV)
    return (xp.reshape(bp, sp, D_MODEL), xs.reshape(bs_, ss, D_MODEL), dk_p, dv_p, mk_all, mv_all,
            st['fc_p'], st['dk_s'], st['dv_s'], st['gv_s'], st['fc_s'])
```

```python
import functools
import math

import numpy as np
import jax
import jax.numpy as jnp
from jax import lax
from jax.experimental import pallas as pl
from jax.experimental.pallas import tpu as pltpu

F32 = jnp.float32
BF16 = jnp.bfloat16

D_MODEL = 1024
CHUNK = 64
DA_HEADS = 4
DA_DK = 64
DA_DV = 128
DA_QK_COLS = DA_HEADS * 2 * DA_DK
DA_WIDTH = DA_HEADS * DA_DV
ROT_DIM = 16
ROT_HALF = ROT_DIM // 2
ROPE_THETA = 500000.0
GM_GROUPS = 4
GM_CH = 128
GM_WIDTH = GM_GROUPS * GM_CH
GM_CHUNK = 128
MEM_LEN = 256
X_HEADS = 4
X_DH = 256
D_FF = 2816
CONV_W = 3
EPS = 1e-6
IN_COLS = 2 * DA_QK_COLS + DA_WIDTH + 2 * GM_WIDTH
NEG = float(np.finfo(np.float32).min)
Q_SCALE = (DA_DK ** -0.5) * math.log2(math.e)

LANES = 128
SUBLANES = 8
VMEM_LIMIT_BYTES = 56 * 1024 * 1024
FF_CHUNK = 256
TOKEN_TILE = 512
WIDE_TILE = 1024
Q_TILE = 256
DA_HEADS_PER_STEP = 2
SUB_ROWS = 512

_NT = (((1,), (1,)), ((), ()))

MEM_ROWS = MEM_LEN * (X_DH // LANES) * X_HEADS


def _mem_rows_view(a):
    lead = a.shape[:-3]
    a = a.reshape(*lead, MEM_LEN, X_HEADS, X_DH // LANES, LANES)
    return jnp.swapaxes(a, -3, -2).reshape(-1, LANES)


def _mem_rows_unview(a, lead):
    a = a.reshape(*lead, MEM_LEN, X_DH // LANES, X_HEADS, LANES)
    return jnp.swapaxes(a, -3, -2).reshape(*lead, MEM_LEN, X_HEADS, X_DH)


def _mem_head_rows(h, c, base=0):
    return pl.ds(base + c * X_HEADS + h, MEM_LEN, stride=(X_DH // LANES) * X_HEADS)


def _params(sem):
    return pltpu.CompilerParams(dimension_semantics=sem, vmem_limit_bytes=VMEM_LIMIT_BYTES)


def _const_spec(shape):
    n = len(shape)
    return pl.BlockSpec(shape, lambda *_: (0,) * n)


_ANY = pl.BlockSpec(memory_space=pl.ANY)


def _rms(x, g):
    return (x * lax.rsqrt(jnp.mean(x * x, axis=-1, keepdims=True) + EPS)) * g


def _gelu(x):
    return x * (0.5 * (1.0 + jnp.tanh(0.7978845608028654 * (x + 0.044715 * (x * x * x)))))


def _qk_slab(t, gain, cosv, sa, sb):
    lo = lax.broadcasted_iota(jnp.int32, (1, LANES), 1) < DA_DK
    sq = t * t
    s_lo = jnp.sum(jnp.where(lo, sq, 0.0), axis=-1, keepdims=True)
    s_hi = jnp.sum(jnp.where(lo, 0.0, sq), axis=-1, keepdims=True)
    ms = jnp.where(lo, s_lo, s_hi) * (1.0 / DA_DK)
    y = (t * lax.rsqrt(ms + EPS)) * gain
    return y * cosv + pltpu.roll(y, LANES - ROT_HALF, 1) * sa + pltpu.roll(y, ROT_HALF, 1) * sb


def _gmlp_cols(z, c0, gmg_ref, u_ref, gv_ref, rows=slice(None)):
    u_ref[rows, :] = _gelu(z[:, c0:c0 + GM_WIDTH]).astype(u_ref.dtype)
    c1 = c0 + GM_WIDTH
    for s in range(GM_GROUPS):
        cols = slice(s * GM_CH, (s + 1) * GM_CH)
        t = _gelu(z[:, c1 + s * GM_CH:c1 + (s + 1) * GM_CH])
        gv_ref[rows, cols] = _rms(t, gmg_ref[:, cols]).astype(gv_ref.dtype)


def _mix_in_sample_body(x_ref, g_ref, w_ref, qg_ref, kg_ref, gmg_ref, cos_ref, sa_ref, sb_ref,
                        q_ref, k_ref, v_ref, u_ref, gv_ref):
    h = _rms(x_ref[...], g_ref[...]).astype(BF16)
    z = jnp.dot(h, w_ref[...], preferred_element_type=F32)
    cosv, sa, sb = cos_ref[...], sa_ref[...], sb_ref[...]
    for s in range(DA_HEADS):
        cols = slice(s * LANES, (s + 1) * LANES)
        q_ref[:, cols] = (_qk_slab(z[:, cols], qg_ref[...], cosv, sa, sb) * Q_SCALE).astype(q_ref.dtype)
        kc = slice(DA_QK_COLS + s * LANES, DA_QK_COLS + (s + 1) * LANES)
        k_ref[:, cols] = _qk_slab(z[:, kc], kg_ref[...], cosv, sa, sb)
    c2 = 2 * DA_QK_COLS
    v_ref[...] = z[:, c2:c2 + DA_WIDTH]
    _gmlp_cols(z, c2 + DA_WIDTH, gmg_ref, u_ref, gv_ref)


def _mix_in_sample(x, g, w, qg, kg, gmg, rope, name):
    t = x.shape[0]
    row = pl.BlockSpec((t, DA_QK_COLS), lambda i: (0, 0))
    rope_spec = _const_spec((t, LANES))
    return pl.pallas_call(
        _mix_in_sample_body,
        grid=(1,),
        in_specs=[_const_spec((t, D_MODEL)), _const_spec((1, D_MODEL)), _const_spec((D_MODEL, IN_COLS)),
                  _const_spec((1, LANES)), _const_spec((1, LANES)), _const_spec((1, GM_WIDTH)),
                  rope_spec, rope_spec, rope_spec],
        out_specs=[row] * 5,
        out_shape=[jax.ShapeDtypeStruct((t, DA_QK_COLS), BF16),
                   jax.ShapeDtypeStruct((t, DA_QK_COLS), F32),
                   jax.ShapeDtypeStruct((t, DA_WIDTH), F32),
                   jax.ShapeDtypeStruct((t, GM_WIDTH), BF16),
                   jax.ShapeDtypeStruct((t, GM_WIDTH), F32)],
        compiler_params=_params(("arbitrary",)),
        name=name,
    )(x, g, w, qg, kg, gmg, *rope)


def _mix_in_prompt_body(x_ref, g_ref, w_ref, qg_ref, kg_ref, gmg_ref, cos_ref, sa_ref, sb_ref,
                        cosk_ref, sink_ref, kt_in, v_in, q_ref, kt_ref, v_ref, vb_ref, u_ref, gv_ref):
    del kt_in, v_in
    tm = x_ref.shape[0]
    c_k, c_v = DA_QK_COLS, 2 * DA_QK_COLS

    def project(j):
        rows = slice(j * SUB_ROWS, (j + 1) * SUB_ROWS)
        h = _rms(x_ref[rows, :], g_ref[...]).astype(BF16)
        return jnp.dot(h, w_ref[...], preferred_element_type=F32)

    def finish(j, z):
        rows = slice(j * SUB_ROWS, (j + 1) * SUB_ROWS)
        cosv, sa, sb = cos_ref[rows, :], sa_ref[rows, :], sb_ref[rows, :]
        for s in range(DA_HEADS):
            cols = slice(s * LANES, (s + 1) * LANES)
            q_ref[rows, cols] = (_qk_slab(z[:, cols], qg_ref[...], cosv, sa, sb) * Q_SCALE).astype(q_ref.dtype)
            v_ref[pl.ds(j * SUB_ROWS * DA_HEADS + s, SUB_ROWS, stride=DA_HEADS), :] = (
                z[:, c_v + s * DA_DV:c_v + (s + 1) * DA_DV])
        vb_ref[rows, :] = z[:, c_v:c_v + DA_WIDTH].astype(vb_ref.dtype)
        zk = z[:, c_k:c_k + DA_QK_COLS].T
        ck, sk, kg = cosk_ref[:, rows], sink_ref[:, rows], kg_ref[...]
        for grp in range(2 * DA_HEADS):
            r0 = grp * DA_DK
            t = zk[r0:r0 + DA_DK, :]
            inv = lax.rsqrt(jnp.sum(t * t, axis=0, keepdims=True) * (1.0 / DA_DK) + EPS)
            y = (t * inv) * kg
            ya, yb = y[:ROT_HALF], y[ROT_HALF:ROT_DIM]
            kt_ref[r0:r0 + ROT_HALF, rows] = ya * ck - yb * sk
            kt_ref[r0 + ROT_HALF:r0 + ROT_DIM, rows] = yb * ck + ya * sk
            kt_ref[r0 + ROT_DIM:r0 + DA_DK, rows] = y[ROT_DIM:]
        _gmlp_cols(z, c_v + DA_WIDTH, gmg_ref, u_ref, gv_ref, rows)

    n_sub = tm // SUB_ROWS
    nxt = project(0)
    for j in range(n_sub):
        cur = nxt
        if j + 1 < n_sub:
            nxt = project(j + 1)
        finish(j, cur)


def _mix_in_prompt(x, g, w, qg, kg_col, gmg, rope, rope_k, kt_all, v_all, layer, seq, tm, name):
    t = x.shape[0]
    per_b = seq // tm
    row = lambda i: (i, 0)
    rope_spec = pl.BlockSpec((tm, LANES), lambda i: (i % per_b, 0))
    ropek_spec = pl.BlockSpec((SUBLANES, tm), lambda i: (0, i % per_b))
    half = pl.BlockSpec((tm, DA_QK_COLS), row)
    kt_spec = pl.BlockSpec((None, None, DA_QK_COLS, tm), lambda i: (layer, i // per_b, 0, i % per_b))
    v_spec = pl.BlockSpec((tm * DA_HEADS, DA_DV), lambda i: (layer * (t // tm) + i, 0))
    return pl.pallas_call(
        _mix_in_prompt_body,
        grid=(t // tm,),
        in_specs=[pl.BlockSpec((tm, D_MODEL), row), _const_spec((1, D_MODEL)),
                  _const_spec((D_MODEL, IN_COLS)),
                  _const_spec((1, LANES)), _const_spec((DA_DK, SUB_ROWS)), _const_spec((1, GM_WIDTH)),
                  rope_spec, rope_spec, rope_spec, ropek_spec, ropek_spec, _ANY, _ANY],
        out_specs=[half, kt_spec, v_spec, half, half, half],
        out_shape=[jax.ShapeDtypeStruct((t, DA_QK_COLS), BF16),
                   jax.ShapeDtypeStruct(kt_all.shape, F32),
                   jax.ShapeDtypeStruct(v_all.shape, F32),
                   jax.ShapeDtypeStruct((t, DA_WIDTH), BF16),
                   jax.ShapeDtypeStruct((t, GM_WIDTH), BF16),
                   jax.ShapeDtypeStruct((t, GM_WIDTH), BF16)],
        input_output_aliases={11: 1, 12: 2},
        compiler_params=_params(("parallel",)),
        name=name,
    )(x, g, w, qg, kg_col, gmg, *rope, *rope_k, kt_all, v_all)


def _lambda(lq1, lk1, lq2, lk2, lam_init):
    a = jnp.sum(lq1[...] * lk1[...], axis=-1, keepdims=True)
    b = jnp.sum(lq2[...] * lk2[...], axis=-1, keepdims=True)
    return jnp.exp(a) - jnp.exp(b) + lam_init


def _subln(o, g, lam_init):
    return _rms(o, g) * (1.0 - lam_init)


def _split_maps(q):
    lo = lax.broadcasted_iota(jnp.int32, (1, LANES), 1) < DA_DK
    zero = jnp.zeros_like(q)
    return jnp.where(lo, q, zero), jnp.where(lo, zero, q)


def _da_prompt_body(lq1, lk1, lq2, lk2, subg_ref, q_ref, kt_ref, v_ref, o_ref, *, lam_init, bq):
    seq = q_ref.shape[1]
    lam = _lambda(lq1, lk1, lq2, lk2, lam_init)
    mask = ((lax.broadcasted_iota(jnp.int32, (bq, bq), 1) // CHUNK)
            <= (lax.broadcasted_iota(jnp.int32, (bq, bq), 0) // CHUNK))
    heads = []
    for h in range(DA_HEADS_PER_STEP):
        cols = slice(h * LANES, (h + 1) * LANES)
        heads.append((_split_maps(q_ref[0, :, cols]),
                      kt_ref[cols, :].astype(BF16),
                      jnp.concatenate([v_ref[0, :, cols], jnp.ones((seq, DA_DV), BF16)], axis=1)))

    def scores(h, qi):
        qm, kt, _ = heads[h]
        lo, hi = qi * bq, (qi + 1) * bq
        out = []
        for m in range(2):
            q = qm[m][lo:hi]
            sd = jnp.dot(q, kt[:, lo:hi], preferred_element_type=F32)
            so = jnp.dot(q, kt[:, :lo], preferred_element_type=F32) if qi > 0 else None
            out.append((sd, so))
        return out

    def attend(h, qi, cur):
        va = heads[h][2]
        lo, hi = qi * bq, (qi + 1) * bq
        outs = []
        for m in range(2):
            sd, so = cur[m]
            sd = jnp.where(mask, sd, NEG)
            mx = jnp.max(sd, axis=-1, keepdims=True)
            if qi > 0:
                mx = jnp.maximum(mx, jnp.max(so, axis=-1, keepdims=True))
            ov = jnp.dot(jnp.exp2(sd - mx).astype(BF16), va[lo:hi], preferred_element_type=F32)
            if qi > 0:
                ov = ov + jnp.dot(jnp.exp2(so - mx).astype(BF16), va[:lo], preferred_element_type=F32)
            outs.append(ov[:, :DA_DV] * (1.0 / ov[:, DA_DV:DA_DV + 1]))
        o = outs[0] - lam * outs[1]
        o_ref[0, lo:hi, h * LANES:(h + 1) * LANES] = _subln(o, subg_ref[...], lam_init).astype(o_ref.dtype)

    nq = seq // bq
    nxt = [scores(h, 0) for h in range(DA_HEADS_PER_STEP)]
    for qi in range(nq):
        cur = nxt
        if qi + 1 < nq:
            nxt = [scores(h, qi + 1) for h in range(DA_HEADS_PER_STEP)]
        for h in range(DA_HEADS_PER_STEP):
            attend(h, qi, cur[h])


def _da_prompt(q, kt_all, vb, layer, lams, subg, lam_init, name):
    b, s, _ = q.shape
    width = DA_HEADS_PER_STEP * LANES
    lam_spec = _const_spec((1, DA_DK))
    q_spec = pl.BlockSpec((1, s, width), lambda bi, h: (bi, 0, h))
    kt_spec = pl.BlockSpec((None, None, width, s), lambda bi, h: (layer, bi, h, 0))
    return pl.pallas_call(
        functools.partial(_da_prompt_body, lam_init=lam_init, bq=Q_TILE),
        grid=(b, DA_HEADS // DA_HEADS_PER_STEP),
        in_specs=[lam_spec, lam_spec, lam_spec, lam_spec, _const_spec((1, DA_DV)),
                  q_spec, kt_spec, q_spec],
        out_specs=q_spec,
        out_shape=jax.ShapeDtypeStruct((b, s, DA_WIDTH), BF16),
        compiler_params=_params(("parallel", "parallel")),
        name=name,
    )(*lams, subg, q, kt_all, vb)


def _da_sample_body(lq1, lk1, lq2, lk2, subg_ref, q_ref, ktp_ref, kn_ref, vp_ref, vn_ref, o_ref,
                    *, lam_init, past, sq):
    lam = _lambda(lq1, lk1, lq2, lk2, lam_init)
    q_chunk = (past + lax.broadcasted_iota(jnp.int32, (sq, 1), 0)) // CHUNK
    mask_p = (lax.broadcasted_iota(jnp.int32, (1, past), 1) // CHUNK) <= q_chunk
    mask_n = ((past + lax.broadcasted_iota(jnp.int32, (1, sq), 1)) // CHUNK) <= q_chunk
    for h in range(DA_HEADS):
        cols = slice(h * LANES, (h + 1) * LANES)
        qm = _split_maps(q_ref[0, :, cols])
        ktp = ktp_ref[cols, :].astype(BF16)
        kn = kn_ref[0, :, cols].astype(BF16)
        probs = []
        for m in range(2):
            sp = jnp.where(mask_p, jnp.dot(qm[m], ktp, preferred_element_type=F32), NEG)
            sn = jnp.where(mask_n, lax.dot_general(qm[m], kn, _NT, preferred_element_type=F32), NEG)
            mx = jnp.maximum(jnp.max(sp, axis=-1, keepdims=True), jnp.max(sn, axis=-1, keepdims=True))
            pp = jnp.exp2(sp - mx)
            pn = jnp.exp2(sn - mx)
            inv = 1.0 / (jnp.sum(pp, axis=-1, keepdims=True) + jnp.sum(pn, axis=-1, keepdims=True))
            probs.append((pp * inv, pn * inv))
        ap = (probs[0][0] - lam * probs[1][0]).astype(BF16)
        an = (probs[0][1] - lam * probs[1][1]).astype(BF16)
        vp = vp_ref[pl.ds(h, past, stride=DA_HEADS), :]
        o = (jnp.dot(ap, vp.astype(BF16), preferred_element_type=F32)
             + jnp.dot(an, vn_ref[0, :, cols].astype(BF16), preferred_element_type=F32))
        o_ref[0, :, cols] = _subln(o, subg_ref[...], lam_init).astype(o_ref.dtype)


def _da_sample(q, k_new, v_new, cache_kt, cache_v, layer, lams, subg, lam_init, name):
    b, sq, _ = q.shape
    past = cache_kt.shape[3]
    lam_spec = _const_spec((1, DA_DK))
    new_spec = pl.BlockSpec((1, sq, DA_QK_COLS), lambda bi: (bi, 0, 0))
    ktp_spec = pl.BlockSpec((None, None, DA_QK_COLS, past), lambda bi: (layer, bi, 0, 0))
    vp_spec = pl.BlockSpec((past * DA_HEADS, DA_DV), lambda bi: (layer * b + bi, 0))
    return pl.pallas_call(
        functools.partial(_da_sample_body, lam_init=lam_init, past=past, sq=sq),
        grid=(b,),
        in_specs=[lam_spec, lam_spec, lam_spec, lam_spec, _const_spec((1, DA_DV)),
                  new_spec, ktp_spec, new_spec, vp_spec, new_spec],
        out_specs=new_spec,
        out_shape=jax.ShapeDtypeStruct((b, sq, DA_WIDTH), BF16),
        compiler_params=_params(("parallel",)),
        name=name,
    )(*lams, subg, q, cache_kt, k_new, cache_v, v_new)


def _mem_kv_body(mem_ref, g_ref, wk_ref, wv_ref, kg_ref, k_in, v_in, k_ref, v_ref, kb_ref, vb_ref, *, nb):
    del k_in, v_in
    m = _rms(mem_ref[...], g_ref[...]).astype(BF16)
    k = jnp.dot(m, wk_ref[...], preferred_element_type=F32)
    v = jnp.dot(m, wv_ref[...], preferred_element_type=F32)
    vb_ref[...] = v.astype(vb_ref.dtype)
    for h in range(X_HEADS):
        cols = slice(h * X_DH, (h + 1) * X_DH)
        kn = _rms(k[:, cols], kg_ref[...])
        kb_ref[:, cols] = kn.astype(kb_ref.dtype)
        for bi in range(nb):
            rows = slice(bi * MEM_LEN, (bi + 1) * MEM_LEN)
            for c in range(X_DH // LANES):
                half = slice(c * LANES, (c + 1) * LANES)
                k_ref[_mem_head_rows(h, c, bi * MEM_ROWS), :] = kn[rows, half]
                v_ref[_mem_head_rows(h, c, bi * MEM_ROWS), :] = v[rows, cols][:, half]


def _mem_kv(mem, g, wk, wv, kg, mk_all, mv_all, layer, tm, name):
    t = mem.shape[0]
    nb = tm // MEM_LEN
    blk = pl.BlockSpec((tm, D_MODEL), lambda i: (i, 0))
    out_spec = pl.BlockSpec((nb * MEM_ROWS, LANES), lambda i: (layer * (t // tm) + i, 0))
    return pl.pallas_call(
        functools.partial(_mem_kv_body, nb=nb),
        grid=(t // tm,),
        in_specs=[blk, _const_spec((1, D_MODEL)), _const_spec((D_MODEL, D_MODEL)),
                  _const_spec((D_MODEL, D_MODEL)), _const_spec((1, X_DH)), _ANY, _ANY],
        out_specs=[out_spec, out_spec, blk, blk],
        out_shape=[jax.ShapeDtypeStruct(mk_all.shape, F32), jax.ShapeDtypeStruct(mv_all.shape, F32),
                   jax.ShapeDtypeStruct((t, D_MODEL), BF16), jax.ShapeDtypeStruct((t, D_MODEL), BF16)],
        input_output_aliases={5: 0, 6: 1},
        compiler_params=_params(("parallel",)),
        name=name,
    )(mem, g, wk, wv, kg, mk_all, mv_all)


def _mix_xattn_body(x_ref, o_ref, u_ref, gv_ref, ws_ref, bs_ref, wout_ref, g_ref, wq_ref, qg_ref,
                    mk_ref, mv_ref, wo_ref, xo_ref, cat_ref, att_ref, *, chunk):
    tm = x_ref.shape[0]
    sub = min(SUB_ROWS, tm)
    flat = mk_ref.shape[-1] == D_MODEL
    tri = (lax.broadcasted_iota(jnp.int32, (chunk, chunk), 1)
           <= lax.broadcasted_iota(jnp.int32, (chunk, chunk), 0))
    ws = [jnp.where(tri, ws_ref[g], 0.0).astype(BF16) for g in range(GM_GROUPS)]

    def mix_out(j):
        r0 = j * sub
        rows = slice(r0, r0 + sub)
        cat_ref[rows, :DA_WIDTH] = o_ref[rows, :]
        for g in range(GM_GROUPS):
            cols = slice(g * GM_CH, (g + 1) * GM_CH)
            bias = bs_ref[g]
            for c in range(sub // chunk):
                cr = slice(r0 + c * chunk, r0 + (c + 1) * chunk)
                s = jnp.dot(ws[g], gv_ref[cr, cols].astype(BF16), preferred_element_type=F32) + bias
                cat_ref[cr, DA_WIDTH + g * GM_CH:DA_WIDTH + (g + 1) * GM_CH] = (
                    u_ref[cr, cols].astype(F32) * s).astype(BF16)
        x1 = x_ref[rows, :] + jnp.dot(cat_ref[rows, :], wout_ref[...], preferred_element_type=F32)
        h = _rms(x1, g_ref[...]).astype(BF16)
        return x1, jnp.dot(h, wq_ref[...], preferred_element_type=F32)

    def attend(j, x1, q):
        rows = slice(j * sub, (j + 1) * sub)
        for hh in range(X_HEADS):
            cols = slice(hh * X_DH, (hh + 1) * X_DH)
            if flat:
                mk, mv = mk_ref[:, cols], mv_ref[:, cols]
            else:
                halves = range(X_DH // LANES)
                mk = jnp.concatenate([mk_ref[_mem_head_rows(hh, c), :] for c in halves], axis=1)
                mv = jnp.concatenate([mv_ref[_mem_head_rows(hh, c), :] for c in halves], axis=1)
            qn = (_rms(q[:, cols], qg_ref[...]) * (X_DH ** -0.5)).astype(BF16)
            s = lax.dot_general(qn, mk.astype(BF16), _NT, preferred_element_type=F32)
            p = jnp.exp(s - jnp.max(s, axis=-1, keepdims=True))
            p = p * (1.0 / jnp.sum(p, axis=-1, keepdims=True))
            att_ref[rows, cols] = jnp.dot(p.astype(BF16), mv.astype(BF16),
                                          preferred_element_type=F32).astype(BF16)
        xo_ref[rows, :] = x1 + jnp.dot(att_ref[rows, :], wo_ref[...], preferred_element_type=F32)

    n_sub = tm // sub
    nxt = mix_out(0)
    for j in range(n_sub):
        cur = nxt
        if j + 1 < n_sub:
            nxt = mix_out(j + 1)
        attend(j, *cur)


def _mix_xattn(x, o, u, gv, ws, bs, wout, g, wq, qg, mk, mv, mem_spec, wo, tm, chunk, name):
    t = x.shape[0]
    row = lambda i: (i, 0)
    blk = pl.BlockSpec((tm, D_MODEL), row)
    half = pl.BlockSpec((tm, DA_WIDTH), row)
    sq = _const_spec((D_MODEL, D_MODEL))
    return pl.pallas_call(
        functools.partial(_mix_xattn_body, chunk=chunk),
        grid=(t // tm,),
        in_specs=[blk, half, half, half, _const_spec((GM_GROUPS, chunk, chunk)),
                  _const_spec((GM_GROUPS, chunk, GM_CH)), sq, _const_spec((1, D_MODEL)), sq,
                  _const_spec((1, X_DH)), mem_spec, mem_spec, sq],
        out_specs=blk,
        out_shape=jax.ShapeDtypeStruct((t, D_MODEL), F32),
        scratch_shapes=[pltpu.VMEM((tm, D_MODEL), BF16), pltpu.VMEM((tm, D_MODEL), BF16)],
        compiler_params=_params(("parallel",)),
        name=name,
    )(x, o, u, gv, ws, bs, wout, g, wq, qg, mk, mv, wo)


def _conv_ffn_body(x_ref, g_ref, wup_ref, cw_ref, cb_ref, wdn_ref, hist_ref, xo_ref, cst_ref,
                   gp_ref, carry_ref, act_ref, *, nb, sub, tiles_per_b):
    i = pl.program_id(0)
    tm = x_ref.shape[0]
    n_sub = tm // sub
    r = sub // nb
    chained = tiles_per_b > 1 or n_sub > 1
    first_tile = (i % tiles_per_b) == 0
    n_chunks = D_FF // FF_CHUNK

    def gate(j):
        h = _rms(x_ref[j * sub:(j + 1) * sub, :], g_ref[...]).astype(BF16)
        for c in range(n_chunks):
            cs = slice(c * FF_CHUNK, (c + 1) * FF_CHUNK)
            g = jnp.dot(h, wup_ref[:, cs], preferred_element_type=F32)
            up = jnp.dot(h, wup_ref[:, D_FF + c * FF_CHUNK:D_FF + (c + 1) * FF_CHUNK],
                         preferred_element_type=F32)
            for bi in range(nb):
                lrows = slice(bi * r, (bi + 1) * r)
                rows = slice(j * sub + bi * r, j * sub + (bi + 1) * r)
                if not chained:
                    prev = hist_ref[bi, :, cs]
                elif j == 0:
                    prev = jnp.where(first_tile, hist_ref[bi, :, cs], carry_ref[:, cs])
                else:
                    prev = carry_ref[:, cs]
                gp = gp_ref.at[c]
                gp[6:8, :] = prev
                gp[8:8 + r, :] = g[lrows]
                conv = (cb_ref[:, cs] + cw_ref[0:1, cs] * gp[6:6 + r, :]
                        + cw_ref[1:2, cs] * gp[7:7 + r, :] + cw_ref[2:3, cs] * g[lrows])
                act = conv * (1.0 / (1.0 + jnp.exp(-conv))) * up[lrows]
                act_ref[rows, cs] = act.astype(BF16)
                last2 = gp[6 + r:8 + r, :]
                cst_ref[bi, :, cs] = last2
                if chained:
                    carry_ref[:, cs] = last2

    def down(j):
        rows = slice(j * sub, (j + 1) * sub)
        xo_ref[rows, :] = x_ref[rows, :] + jnp.dot(act_ref[rows, :], wdn_ref[...],
                                                   preferred_element_type=F32)

    gate(0)
    for j in range(n_sub):
        if j + 1 < n_sub:
            gate(j + 1)
        down(j)


def _conv_ffn(x, g, wup, cw, cb, wdn, hist, tm, sub, nb, tiles_per_b, name):
    t = x.shape[0]
    row = lambda i: (i, 0)
    blk = pl.BlockSpec((tm, D_MODEL), row)
    hist_spec = pl.BlockSpec((nb, CONV_W - 1, D_FF), lambda i: (i // tiles_per_b, 0, 0))
    resident = lambda shape: pl.BlockSpec(shape, lambda i: (0, 0), pipeline_mode=pl.Buffered(1))
    return pl.pallas_call(
        functools.partial(_conv_ffn_body, nb=nb, sub=sub, tiles_per_b=tiles_per_b),
        grid=(t // tm,),
        in_specs=[blk, _const_spec((1, D_MODEL)), resident((D_MODEL, 2 * D_FF)),
                  _const_spec((CONV_W, D_FF)), _const_spec((1, D_FF)), resident((D_FF, D_MODEL)),
                  hist_spec],
        out_specs=[blk, hist_spec],
        out_shape=[jax.ShapeDtypeStruct((t, D_MODEL), F32),
                   jax.ShapeDtypeStruct(hist.shape, F32)],
        scratch_shapes=[pltpu.VMEM((D_FF // FF_CHUNK, sub // nb + 8, FF_CHUNK), F32),
                        pltpu.VMEM((CONV_W - 1, D_FF), F32),
                        pltpu.VMEM((tm, D_FF), BF16)],
        compiler_params=_params(("arbitrary",)),
        name=name,
    )(x, g, wup, cw, cb, wdn, hist)


def _rope_angles(pos):
    inv = ROPE_THETA ** (-jnp.arange(ROT_HALF, dtype=F32) / ROT_HALF)
    ang = pos.astype(F32)[:, None] * inv[None, :]
    return jnp.cos(ang), jnp.sin(ang)


def _rope_tables(pos):
    cos, sin = _rope_angles(pos)
    n = pos.shape[0]
    ones = jnp.ones((n, DA_DK - ROT_DIM), F32)
    zeros_h = jnp.zeros((n, ROT_HALF), F32)
    zeros_r = jnp.zeros((n, DA_DK - ROT_DIM), F32)
    cos64 = jnp.concatenate([cos, cos, ones], axis=1)
    sa64 = jnp.concatenate([-sin, zeros_h, zeros_r], axis=1)
    sb64 = jnp.concatenate([zeros_h, sin, zeros_r], axis=1)
    two = lambda a: jnp.concatenate([a, a], axis=1)
    return two(cos64), two(sa64), two(sb64)


def _post_attention(x, o, u, gv, w, mk, mv, mem_spec, hist, seq, tm, tag):
    t = x.shape[0]
    row2 = lambda a: a.reshape(1, -1)
    chunk = min(seq, GM_CHUNK)
    ws = w['gm_w_s'][:, :chunk, :chunk]
    bs = jnp.broadcast_to(w['gm_b'][:, :chunk, None], (GM_GROUPS, chunk, GM_CH))
    x = _mix_xattn(x, o.reshape(t, DA_WIDTH), u, gv, ws, bs, w['w_out'], row2(w['norm_x_g']), w['wq_c'],
                   row2(w['xq_norm_g']), mk, mv, mem_spec, w['wo_c'], min(tm, seq), chunk,
                   f"mix_xattn_{tag}")
    sub = min(SUB_ROWS, tm)
    return _conv_ffn(x, row2(w['norm_ffn_g']), w['w_up'], w['conv_w'], row2(w['conv_b']),
                     w['w_down'], hist, tm, sub, max(1, sub // seq), max(1, seq // tm), f"conv_ffn_{tag}")


def kernel(x_prompt, x_sample, cache_da_k, cache_da_v, cache_mem_k, cache_mem_v, state_ffn_conv, mem_prompt, norm_mix_g, w_in, da_q_norm_g, da_k_norm_g, lambda_q1, lambda_k1, lambda_q2, lambda_k2, da_subln_g, gm_norm_g, gm_w_s, gm_b, w_out, norm_x_g, norm_mem_g, wq_c, wk_c, wv_c, wo_c, xq_norm_g, xk_norm_g, norm_ffn_g, w_up, conv_w, conv_b, w_down):
    bp, sp, _ = x_prompt.shape
    bs_, ss, _ = x_sample.shape
    depth = w_in.shape[0]
    past = cache_da_k.shape[2]
    tm_w = min(WIDE_TILE, sp)
    tm_s = bs_ * ss
    tm_mem = min(TOKEN_TILE, bp * MEM_LEN)
    assert sp % tm_w == 0 and tm_w % SUB_ROWS == 0 and sp % Q_TILE == 0 and SUB_ROWS % GM_CHUNK == 0
    assert ss <= GM_CHUNK and tm_s % SUBLANES == 0 and ss >= CONV_W - 1 and ss % SUBLANES == 0
    assert (bp * MEM_LEN) % tm_mem == 0 and tm_mem % MEM_LEN == 0

    pos_p = jnp.arange(sp)
    rope_p = _rope_tables(pos_p)
    cos_p, sin_p = _rope_angles(pos_p)
    rope_kp = (cos_p.T, sin_p.T)
    rope_s = tuple(jnp.tile(a, (bs_, 1)) for a in _rope_tables(past + jnp.arange(ss)))
    cache_kt = jnp.transpose(cache_da_k, (0, 1, 3, 4, 5, 2)).reshape(depth, bs_, DA_QK_COLS, past)
    cache_v2 = cache_da_v.reshape(depth * bs_ * past * DA_HEADS, DA_DV)
    hist_p = jnp.zeros((bp, CONV_W - 1, D_FF), F32)
    mem_flat = mem_prompt.reshape(bp * MEM_LEN, D_MODEL)

    kt_all = lax.empty((depth, bp, DA_QK_COLS, sp), F32)
    v_all = lax.empty((depth * bp * sp * DA_HEADS, DA_DV), F32)
    mk_all = lax.empty((depth * bp * MEM_ROWS, LANES), F32)
    mv_all = lax.empty((depth * bp * MEM_ROWS, LANES), F32)
    cmk, cmv = _mem_rows_view(cache_mem_k), _mem_rows_view(cache_mem_v)

    xp = x_prompt.reshape(bp * sp, D_MODEL)
    xs = x_sample.reshape(tm_s, D_MODEL)
    outs = {n: [] for n in ('fc_p', 'dk_s', 'dv_s', 'gv_s', 'fc_s')}
    row2 = lambda a: a.reshape(1, -1)
    for l in range(depth):
        lam_init = 0.8 - 0.6 * math.exp(-0.3 * l)
        w = dict(gm_w_s=gm_w_s[l], gm_b=gm_b[l], w_out=w_out[l].astype(BF16),
                 norm_x_g=norm_x_g[l], wq_c=wq_c[l].astype(BF16), wo_c=wo_c[l].astype(BF16),
                 xq_norm_g=xq_norm_g[l], norm_ffn_g=norm_ffn_g[l], w_up=w_up[l].astype(BF16),
                 conv_w=conv_w[l], conv_b=conv_b[l], w_down=w_down[l].astype(BF16))
        w_in_l = w_in[l].astype(BF16)
        qg = jnp.tile(da_q_norm_g[l], 2).reshape(1, -1)
        kg = jnp.tile(da_k_norm_g[l], 2).reshape(1, -1)
        kg_col = jnp.broadcast_to(da_k_norm_g[l][:, None], (DA_DK, SUB_ROWS))
        gmg = jnp.tile(gm_norm_g[l], GM_GROUPS).reshape(1, -1)
        lams = [row2(a[l]) for a in (lambda_q1, lambda_k1, lambda_q2, lambda_k2)]
        subg = row2(da_subln_g[l])
        g_mix = row2(norm_mix_g[l])

        mk_all, mv_all, mkb, mvb = _mem_kv(mem_flat, row2(norm_mem_g[l]), wk_c[l].astype(BF16),
                                           wv_c[l].astype(BF16), row2(xk_norm_g[l]), mk_all, mv_all, l,
                                           tm_mem, f"mem_kv_{l}")
        q, kt_all, v_all, vb, u, gv = _mix_in_prompt(xp, g_mix, w_in_l, qg, kg_col, gmg, rope_p, rope_kp,
                                                     kt_all, v_all, l, sp, tm_w, f"mix_in_p{l}")
        o = _da_prompt(q.reshape(bp, sp, DA_QK_COLS), kt_all, vb.reshape(bp, sp, DA_WIDTH), l, lams, subg,
                       lam_init, f"da_p{l}")
        mem_spec_p = pl.BlockSpec((MEM_LEN, D_MODEL), lambda i: (i // (sp // tm_w), 0))
        xp, cst = _post_attention(xp, o, u, gv, w, mkb, mvb, mem_spec_p, hist_p, sp, tm_w, f"p{l}")
        outs['fc_p'].append(cst)

        q, k, v, u, gv = _mix_in_sample(xs, g_mix, w_in_l, qg, kg, gmg, rope_s, f"mix_in_s{l}")
        o = _da_sample(q.reshape(bs_, ss, DA_QK_COLS), k.reshape(bs_, ss, DA_QK_COLS),
                       v.reshape(bs_, ss, DA_WIDTH), cache_kt, cache_v2, l, lams, subg, lam_init,
                       f"da_s{l}")
        mem_spec_s = pl.BlockSpec((MEM_ROWS, LANES), lambda i, l=l: (l * bs_ + i, 0))
        xs, cst = _post_attention(xs, o, u, gv, w, cmk, cmv, mem_spec_s, state_ffn_conv[l],
                                  ss, tm_s, f"s{l}")
        outs['dk_s'].append(k.reshape(bs_, ss, DA_HEADS, 2, DA_DK))
        outs['dv_s'].append(v.reshape(bs_, ss, DA_HEADS, DA_DV))
        outs['gv_s'].append(gv.reshape(bs_, ss, GM_GROUPS, GM_CH))
        outs['fc_s'].append(cst)

    st = {n: jnp.stack(a) for n, a in outs.items()}
    dk_p = jnp.transpose(kt_all.reshape(depth, bp, DA_HEADS, 2, DA_DK, sp), (0, 1, 5, 2, 3, 4))
    dv_p = v_all.reshape(depth, bp, sp, DA_HEADS, DA_DV)
    mk_p, mv_p = (_mem_rows_unview(a, (depth, bp)) for a in (mk_all, mv_all))
    return (xp.reshape(bp, sp, D_MODEL), xs.reshape(bs_, ss, D_MODEL), dk_p, dv_p, mk_p, mv_p,
            st['fc_p'], st['dk_s'], st['dv_s'], st['gv_s'], st['fc_s'])
```

```python
import functools
import math

import numpy as np
import jax
import jax.numpy as jnp
from jax import lax
from jax.experimental import pallas as pl
from jax.experimental.pallas import tpu as pltpu

F32 = jnp.float32
BF16 = jnp.bfloat16

D_MODEL = 1024
CHUNK = 64
DA_HEADS = 4
DA_DK = 64
DA_DV = 128
DA_QK_COLS = DA_HEADS * 2 * DA_DK
DA_WIDTH = DA_HEADS * DA_DV
ROT_DIM = 16
ROT_HALF = ROT_DIM // 2
ROPE_THETA = 500000.0
GM_GROUPS = 4
GM_CH = 128
GM_WIDTH = GM_GROUPS * GM_CH
GM_CHUNK = 128
MEM_LEN = 256
X_HEADS = 4
X_DH = 256
D_FF = 2816
CONV_W = 3
EPS = 1e-6
IN_COLS = 2 * DA_QK_COLS + DA_WIDTH + 2 * GM_WIDTH
NEG = float(np.finfo(np.float32).min)
Q_SCALE = (DA_DK ** -0.5) * math.log2(math.e)

LANES = 128
SUBLANES = 8
VMEM_LIMIT_BYTES = 56 * 1024 * 1024
FF_CHUNK = 256
TOKEN_TILE = 512
WIDE_TILE = 1024
Q_TILE = 256
DA_HEADS_PER_STEP = 2
SUB_ROWS = 512

_NT = (((1,), (1,)), ((), ()))

MEM_ROWS = MEM_LEN * (X_DH // LANES) * X_HEADS


def _mem_rows_view(a):
    lead = a.shape[:-3]
    a = a.reshape(*lead, MEM_LEN, X_HEADS, X_DH // LANES, LANES)
    return jnp.swapaxes(a, -3, -2).reshape(-1, LANES)


def _mem_rows_unview(a, lead):
    a = a.reshape(*lead, MEM_LEN, X_DH // LANES, X_HEADS, LANES)
    return jnp.swapaxes(a, -3, -2).reshape(*lead, MEM_LEN, X_HEADS, X_DH)


def _mem_head_rows(h, c, base=0):
    return pl.ds(base + c * X_HEADS + h, MEM_LEN, stride=(X_DH // LANES) * X_HEADS)


def _params(sem):
    return pltpu.CompilerParams(dimension_semantics=sem, vmem_limit_bytes=VMEM_LIMIT_BYTES)


def _const_spec(shape):
    n = len(shape)
    return pl.BlockSpec(shape, lambda *_: (0,) * n)


_ANY = pl.BlockSpec(memory_space=pl.ANY)


def _layer_spec(wl, buffers=None):
    w, l = wl
    kw = {} if buffers is None else dict(pipeline_mode=pl.Buffered(buffers))
    return pl.BlockSpec((None,) + w.shape[1:], lambda *_: (l, 0, 0), **kw)


def _rms(x, g):
    return (x * lax.rsqrt(jnp.mean(x * x, axis=-1, keepdims=True) + EPS)) * g


def _gelu(x):
    return x * (0.5 * (1.0 + jnp.tanh(0.7978845608028654 * (x + 0.044715 * (x * x * x)))))


def _qk_slab(t, gain, cosv, sa, sb):
    lo = lax.broadcasted_iota(jnp.int32, (1, LANES), 1) < DA_DK
    sq = t * t
    s_lo = jnp.sum(jnp.where(lo, sq, 0.0), axis=-1, keepdims=True)
    s_hi = jnp.sum(jnp.where(lo, 0.0, sq), axis=-1, keepdims=True)
    ms = jnp.where(lo, s_lo, s_hi) * (1.0 / DA_DK)
    y = (t * lax.rsqrt(ms + EPS)) * gain
    return y * cosv + pltpu.roll(y, LANES - ROT_HALF, 1) * sa + pltpu.roll(y, ROT_HALF, 1) * sb


def _gmlp_cols(z, c0, gmg_ref, u_ref, gv_ref, rows=slice(None)):
    u_ref[rows, :] = _gelu(z[:, c0:c0 + GM_WIDTH]).astype(u_ref.dtype)
    c1 = c0 + GM_WIDTH
    for s in range(GM_GROUPS):
        cols = slice(s * GM_CH, (s + 1) * GM_CH)
        t = _gelu(z[:, c1 + s * GM_CH:c1 + (s + 1) * GM_CH])
        gv_ref[rows, cols] = _rms(t, gmg_ref[:, cols]).astype(gv_ref.dtype)


def _mix_in_sample_body(x_ref, g_ref, w_ref, qg_ref, kg_ref, gmg_ref, cos_ref, sa_ref, sb_ref,
                        q_ref, k_ref, v_ref, u_ref, gv_ref):
    h = _rms(x_ref[...], g_ref[...]).astype(BF16)
    z = jnp.dot(h, w_ref[...], preferred_element_type=F32)
    cosv, sa, sb = cos_ref[...], sa_ref[...], sb_ref[...]
    for s in range(DA_HEADS):
        cols = slice(s * LANES, (s + 1) * LANES)
        q_ref[:, cols] = (_qk_slab(z[:, cols], qg_ref[...], cosv, sa, sb) * Q_SCALE).astype(q_ref.dtype)
        kc = slice(DA_QK_COLS + s * LANES, DA_QK_COLS + (s + 1) * LANES)
        k_ref[:, cols] = _qk_slab(z[:, kc], kg_ref[...], cosv, sa, sb)
    c2 = 2 * DA_QK_COLS
    v_ref[...] = z[:, c2:c2 + DA_WIDTH]
    _gmlp_cols(z, c2 + DA_WIDTH, gmg_ref, u_ref, gv_ref)


def _mix_in_sample(x, g, w, qg, kg, gmg, rope, name):
    t = x.shape[0]
    row = pl.BlockSpec((t, DA_QK_COLS), lambda i: (0, 0))
    rope_spec = _const_spec((t, LANES))
    return pl.pallas_call(
        _mix_in_sample_body,
        grid=(1,),
        in_specs=[_const_spec((t, D_MODEL)), _const_spec((1, D_MODEL)), _layer_spec(w),
                  _const_spec((1, LANES)), _const_spec((1, LANES)), _const_spec((1, GM_WIDTH)),
                  rope_spec, rope_spec, rope_spec],
        out_specs=[row] * 5,
        out_shape=[jax.ShapeDtypeStruct((t, DA_QK_COLS), BF16),
                   jax.ShapeDtypeStruct((t, DA_QK_COLS), F32),
                   jax.ShapeDtypeStruct((t, DA_WIDTH), F32),
                   jax.ShapeDtypeStruct((t, GM_WIDTH), BF16),
                   jax.ShapeDtypeStruct((t, GM_WIDTH), F32)],
        compiler_params=_params(("arbitrary",)),
        name=name,
    )(x, g, w[0], qg, kg, gmg, *rope)


def _mix_in_prompt_body(x_ref, g_ref, w_ref, qg_ref, kg_ref, gmg_ref, cos_ref, sa_ref, sb_ref,
                        cosk_ref, sink_ref, kt_in, v_in, q_ref, kt_ref, v_ref, vb_ref, u_ref, gv_ref):
    del kt_in, v_in
    tm = x_ref.shape[0]
    c_k, c_v = DA_QK_COLS, 2 * DA_QK_COLS

    def project(j):
        rows = slice(j * SUB_ROWS, (j + 1) * SUB_ROWS)
        h = _rms(x_ref[rows, :], g_ref[...]).astype(BF16)
        return jnp.dot(h, w_ref[...], preferred_element_type=F32)

    def finish(j, z):
        rows = slice(j * SUB_ROWS, (j + 1) * SUB_ROWS)
        cosv, sa, sb = cos_ref[rows, :], sa_ref[rows, :], sb_ref[rows, :]
        for s in range(DA_HEADS):
            cols = slice(s * LANES, (s + 1) * LANES)
            q_ref[rows, cols] = (_qk_slab(z[:, cols], qg_ref[...], cosv, sa, sb) * Q_SCALE).astype(q_ref.dtype)
            v_ref[pl.ds(j * SUB_ROWS * DA_HEADS + s, SUB_ROWS, stride=DA_HEADS), :] = (
                z[:, c_v + s * DA_DV:c_v + (s + 1) * DA_DV])
        vb_ref[rows, :] = z[:, c_v:c_v + DA_WIDTH].astype(vb_ref.dtype)
        zk = z[:, c_k:c_k + DA_QK_COLS].T
        ck, sk, kg = cosk_ref[:, rows], sink_ref[:, rows], kg_ref[...]
        for grp in range(2 * DA_HEADS):
            r0 = grp * DA_DK
            t = zk[r0:r0 + DA_DK, :]
            inv = lax.rsqrt(jnp.sum(t * t, axis=0, keepdims=True) * (1.0 / DA_DK) + EPS)
            y = (t * inv) * kg
            ya, yb = y[:ROT_HALF], y[ROT_HALF:ROT_DIM]
            kt_ref[r0:r0 + ROT_HALF, rows] = ya * ck - yb * sk
            kt_ref[r0 + ROT_HALF:r0 + ROT_DIM, rows] = yb * ck + ya * sk
            kt_ref[r0 + ROT_DIM:r0 + DA_DK, rows] = y[ROT_DIM:]
        _gmlp_cols(z, c_v + DA_WIDTH, gmg_ref, u_ref, gv_ref, rows)

    n_sub = tm // SUB_ROWS
    nxt = project(0)
    for j in range(n_sub):
        cur = nxt
        if j + 1 < n_sub:
            nxt = project(j + 1)
        finish(j, cur)


def _mix_in_prompt(x, g, w, qg, kg_col, gmg, rope, rope_k, kt_all, v_all, layer, seq, tm, name):
    t = x.shape[0]
    per_b = seq // tm
    row = lambda i: (i, 0)
    rope_spec = pl.BlockSpec((tm, LANES), lambda i: (i % per_b, 0))
    ropek_spec = pl.BlockSpec((SUBLANES, tm), lambda i: (0, i % per_b))
    half = pl.BlockSpec((tm, DA_QK_COLS), row)
    kt_spec = pl.BlockSpec((None, None, DA_QK_COLS, tm), lambda i: (layer, i // per_b, 0, i % per_b))
    v_spec = pl.BlockSpec((tm * DA_HEADS, DA_DV), lambda i: (layer * (t // tm) + i, 0))
    return pl.pallas_call(
        _mix_in_prompt_body,
        grid=(t // tm,),
        in_specs=[pl.BlockSpec((tm, D_MODEL), row), _const_spec((1, D_MODEL)),
                  _layer_spec(w),
                  _const_spec((1, LANES)), _const_spec((DA_DK, SUB_ROWS)), _const_spec((1, GM_WIDTH)),
                  rope_spec, rope_spec, rope_spec, ropek_spec, ropek_spec, _ANY, _ANY],
        out_specs=[half, kt_spec, v_spec, half, half, half],
        out_shape=[jax.ShapeDtypeStruct((t, DA_QK_COLS), BF16),
                   jax.ShapeDtypeStruct(kt_all.shape, F32),
                   jax.ShapeDtypeStruct(v_all.shape, F32),
                   jax.ShapeDtypeStruct((t, DA_WIDTH), BF16),
                   jax.ShapeDtypeStruct((t, GM_WIDTH), BF16),
                   jax.ShapeDtypeStruct((t, GM_WIDTH), BF16)],
        input_output_aliases={11: 1, 12: 2},
        compiler_params=_params(("parallel",)),
        name=name,
    )(x, g, w[0], qg, kg_col, gmg, *rope, *rope_k, kt_all, v_all)


def _lambda(lq1, lk1, lq2, lk2, lam_init):
    a = jnp.sum(lq1[...] * lk1[...], axis=-1, keepdims=True)
    b = jnp.sum(lq2[...] * lk2[...], axis=-1, keepdims=True)
    return jnp.exp(a) - jnp.exp(b) + lam_init


def _subln(o, g, lam_init):
    return _rms(o, g) * (1.0 - lam_init)


def _split_maps(q):
    lo = lax.broadcasted_iota(jnp.int32, (1, LANES), 1) < DA_DK
    zero = jnp.zeros_like(q)
    return jnp.where(lo, q, zero), jnp.where(lo, zero, q)


def _da_prompt_body(lq1, lk1, lq2, lk2, subg_ref, q_ref, kt_ref, v_ref, o_ref, *, lam_init, bq):
    seq = q_ref.shape[1]
    lam = _lambda(lq1, lk1, lq2, lk2, lam_init)
    mask = ((lax.broadcasted_iota(jnp.int32, (bq, bq), 1) // CHUNK)
            <= (lax.broadcasted_iota(jnp.int32, (bq, bq), 0) // CHUNK))
    heads = []
    for h in range(DA_HEADS_PER_STEP):
        cols = slice(h * LANES, (h + 1) * LANES)
        heads.append((_split_maps(q_ref[0, :, cols]),
                      kt_ref[cols, :].astype(BF16),
                      jnp.concatenate([v_ref[0, :, cols], jnp.ones((seq, DA_DV), BF16)], axis=1)))

    def scores(h, qi):
        qm, kt, _ = heads[h]
        lo, hi = qi * bq, (qi + 1) * bq
        out = []
        for m in range(2):
            q = qm[m][lo:hi]
            sd = jnp.dot(q, kt[:, lo:hi], preferred_element_type=F32)
            so = jnp.dot(q, kt[:, :lo], preferred_element_type=F32) if qi > 0 else None
            out.append((sd, so))
        return out

    def attend(h, qi, cur):
        va = heads[h][2]
        lo, hi = qi * bq, (qi + 1) * bq
        outs = []
        for m in range(2):
            sd, so = cur[m]
            sd = jnp.where(mask, sd, NEG)
            mx = jnp.max(sd, axis=-1, keepdims=True)
            if qi > 0:
                mx = jnp.maximum(mx, jnp.max(so, axis=-1, keepdims=True))
            ov = jnp.dot(jnp.exp2(sd - mx).astype(BF16), va[lo:hi], preferred_element_type=F32)
            if qi > 0:
                ov = ov + jnp.dot(jnp.exp2(so - mx).astype(BF16), va[:lo], preferred_element_type=F32)
            outs.append(ov[:, :DA_DV] * (1.0 / ov[:, DA_DV:DA_DV + 1]))
        o = outs[0] - lam * outs[1]
        o_ref[0, lo:hi, h * LANES:(h + 1) * LANES] = _subln(o, subg_ref[...], lam_init).astype(o_ref.dtype)

    nq = seq // bq
    nxt = [scores(h, 0) for h in range(DA_HEADS_PER_STEP)]
    for qi in range(nq):
        cur = nxt
        if qi + 1 < nq:
            nxt = [scores(h, qi + 1) for h in range(DA_HEADS_PER_STEP)]
        for h in range(DA_HEADS_PER_STEP):
            attend(h, qi, cur[h])


def _da_prompt(q, kt_all, vb, layer, lams, subg, lam_init, name):
    b, s, _ = q.shape
    width = DA_HEADS_PER_STEP * LANES
    lam_spec = _const_spec((1, DA_DK))
    q_spec = pl.BlockSpec((1, s, width), lambda bi, h: (bi, 0, h))
    kt_spec = pl.BlockSpec((None, None, width, s), lambda bi, h: (layer, bi, h, 0))
    return pl.pallas_call(
        functools.partial(_da_prompt_body, lam_init=lam_init, bq=Q_TILE),
        grid=(b, DA_HEADS // DA_HEADS_PER_STEP),
        in_specs=[lam_spec, lam_spec, lam_spec, lam_spec, _const_spec((1, DA_DV)),
                  q_spec, kt_spec, q_spec],
        out_specs=q_spec,
        out_shape=jax.ShapeDtypeStruct((b, s, DA_WIDTH), BF16),
        compiler_params=_params(("parallel", "parallel")),
        name=name,
    )(*lams, subg, q, kt_all, vb)


def _da_sample_body(lq1, lk1, lq2, lk2, subg_ref, q_ref, ktp_ref, kn_ref, vp_ref, vn_ref, o_ref,
                    *, lam_init, past, sq):
    lam = _lambda(lq1, lk1, lq2, lk2, lam_init)
    q_chunk = (past + lax.broadcasted_iota(jnp.int32, (sq, 1), 0)) // CHUNK
    mask_p = (lax.broadcasted_iota(jnp.int32, (1, past), 1) // CHUNK) <= q_chunk
    mask_n = ((past + lax.broadcasted_iota(jnp.int32, (1, sq), 1)) // CHUNK) <= q_chunk
    for h in range(DA_HEADS):
        cols = slice(h * LANES, (h + 1) * LANES)
        qm = _split_maps(q_ref[0, :, cols])
        ktp = ktp_ref[cols, :].astype(BF16)
        kn = kn_ref[0, :, cols].astype(BF16)
        probs = []
        for m in range(2):
            sp = jnp.where(mask_p, jnp.dot(qm[m], ktp, preferred_element_type=F32), NEG)
            sn = jnp.where(mask_n, lax.dot_general(qm[m], kn, _NT, preferred_element_type=F32), NEG)
            mx = jnp.maximum(jnp.max(sp, axis=-1, keepdims=True), jnp.max(sn, axis=-1, keepdims=True))
            pp = jnp.exp2(sp - mx)
            pn = jnp.exp2(sn - mx)
            inv = 1.0 / (jnp.sum(pp, axis=-1, keepdims=True) + jnp.sum(pn, axis=-1, keepdims=True))
            probs.append((pp * inv, pn * inv))
        ap = (probs[0][0] - lam * probs[1][0]).astype(BF16)
        an = (probs[0][1] - lam * probs[1][1]).astype(BF16)
        vp = vp_ref[pl.ds(h, past, stride=DA_HEADS), :]
        o = (jnp.dot(ap, vp.astype(BF16), preferred_element_type=F32)
             + jnp.dot(an, vn_ref[0, :, cols].astype(BF16), preferred_element_type=F32))
        o_ref[0, :, cols] = _subln(o, subg_ref[...], lam_init).astype(o_ref.dtype)


def _da_sample(q, k_new, v_new, cache_kt, cache_v, layer, lams, subg, lam_init, name):
    b, sq, _ = q.shape
    past = cache_kt.shape[3]
    lam_spec = _const_spec((1, DA_DK))
    new_spec = pl.BlockSpec((1, sq, DA_QK_COLS), lambda bi: (bi, 0, 0))
    ktp_spec = pl.BlockSpec((None, None, DA_QK_COLS, past), lambda bi: (layer, bi, 0, 0))
    vp_spec = pl.BlockSpec((past * DA_HEADS, DA_DV), lambda bi: (layer * b + bi, 0))
    return pl.pallas_call(
        functools.partial(_da_sample_body, lam_init=lam_init, past=past, sq=sq),
        grid=(b,),
        in_specs=[lam_spec, lam_spec, lam_spec, lam_spec, _const_spec((1, DA_DV)),
                  new_spec, ktp_spec, new_spec, vp_spec, new_spec],
        out_specs=new_spec,
        out_shape=jax.ShapeDtypeStruct((b, sq, DA_WIDTH), BF16),
        compiler_params=_params(("parallel",)),
        name=name,
    )(*lams, subg, q, cache_kt, k_new, cache_v, v_new)


def _mem_kv_body(mem_ref, g_ref, wk_ref, wv_ref, kg_ref, k_in, v_in, k_ref, v_ref, kb_ref, vb_ref, *, nb):
    del k_in, v_in
    m = _rms(mem_ref[...], g_ref[...]).astype(BF16)
    k = jnp.dot(m, wk_ref[...], preferred_element_type=F32)
    v = jnp.dot(m, wv_ref[...], preferred_element_type=F32)
    vb_ref[...] = v.astype(vb_ref.dtype)
    for h in range(X_HEADS):
        cols = slice(h * X_DH, (h + 1) * X_DH)
        kn = _rms(k[:, cols], kg_ref[...])
        kb_ref[:, cols] = kn.astype(kb_ref.dtype)
        for bi in range(nb):
            rows = slice(bi * MEM_LEN, (bi + 1) * MEM_LEN)
            for c in range(X_DH // LANES):
                half = slice(c * LANES, (c + 1) * LANES)
                k_ref[_mem_head_rows(h, c, bi * MEM_ROWS), :] = kn[rows, half]
                v_ref[_mem_head_rows(h, c, bi * MEM_ROWS), :] = v[rows, cols][:, half]


def _mem_kv(mem, g, wk, wv, kg, mk_all, mv_all, layer, tm, name):
    t = mem.shape[0]
    nb = tm // MEM_LEN
    blk = pl.BlockSpec((tm, D_MODEL), lambda i: (i, 0))
    out_spec = pl.BlockSpec((nb * MEM_ROWS, LANES), lambda i: (layer * (t // tm) + i, 0))
    return pl.pallas_call(
        functools.partial(_mem_kv_body, nb=nb),
        grid=(t // tm,),
        in_specs=[blk, _const_spec((1, D_MODEL)), _layer_spec(wk), _layer_spec(wv),
                  _const_spec((1, X_DH)), _ANY, _ANY],
        out_specs=[out_spec, out_spec, blk, blk],
        out_shape=[jax.ShapeDtypeStruct(mk_all.shape, F32), jax.ShapeDtypeStruct(mv_all.shape, F32),
                   jax.ShapeDtypeStruct((t, D_MODEL), BF16), jax.ShapeDtypeStruct((t, D_MODEL), BF16)],
        input_output_aliases={5: 0, 6: 1},
        compiler_params=_params(("parallel",)),
        name=name,
    )(mem, g, wk[0], wv[0], kg, mk_all, mv_all)


def _mix_xattn_body(x_ref, o_ref, u_ref, gv_ref, ws_ref, bs_ref, wout_ref, g_ref, wq_ref, qg_ref,
                    mk_ref, mv_ref, wo_ref, xo_ref, cat_ref, att_ref, *, chunk):
    tm = x_ref.shape[0]
    sub = min(SUB_ROWS, tm)
    flat = mk_ref.shape[-1] == D_MODEL
    tri = (lax.broadcasted_iota(jnp.int32, (chunk, chunk), 1)
           <= lax.broadcasted_iota(jnp.int32, (chunk, chunk), 0))
    ws = [jnp.where(tri, ws_ref[g], 0.0).astype(BF16) for g in range(GM_GROUPS)]

    def mix_out(j):
        r0 = j * sub
        rows = slice(r0, r0 + sub)
        cat_ref[rows, :DA_WIDTH] = o_ref[rows, :]
        for g in range(GM_GROUPS):
            cols = slice(g * GM_CH, (g + 1) * GM_CH)
            bias = bs_ref[g]
            for c in range(sub // chunk):
                cr = slice(r0 + c * chunk, r0 + (c + 1) * chunk)
                s = jnp.dot(ws[g], gv_ref[cr, cols].astype(BF16), preferred_element_type=F32) + bias
                cat_ref[cr, DA_WIDTH + g * GM_CH:DA_WIDTH + (g + 1) * GM_CH] = (
                    u_ref[cr, cols].astype(F32) * s).astype(BF16)
        x1 = x_ref[rows, :] + jnp.dot(cat_ref[rows, :], wout_ref[...], preferred_element_type=F32)
        h = _rms(x1, g_ref[...]).astype(BF16)
        return x1, jnp.dot(h, wq_ref[...], preferred_element_type=F32)

    def attend(j, x1, q):
        rows = slice(j * sub, (j + 1) * sub)
        for hh in range(X_HEADS):
            cols = slice(hh * X_DH, (hh + 1) * X_DH)
            if flat:
                mk, mv = mk_ref[:, cols], mv_ref[:, cols]
            else:
                halves = range(X_DH // LANES)
                mk = jnp.concatenate([mk_ref[_mem_head_rows(hh, c), :] for c in halves], axis=1)
                mv = jnp.concatenate([mv_ref[_mem_head_rows(hh, c), :] for c in halves], axis=1)
            qn = (_rms(q[:, cols], qg_ref[...]) * (X_DH ** -0.5)).astype(BF16)
            s = lax.dot_general(qn, mk.astype(BF16), _NT, preferred_element_type=F32)
            p = jnp.exp(s - jnp.max(s, axis=-1, keepdims=True))
            p = p * (1.0 / jnp.sum(p, axis=-1, keepdims=True))
            att_ref[rows, cols] = jnp.dot(p.astype(BF16), mv.astype(BF16),
                                          preferred_element_type=F32).astype(BF16)
        xo_ref[rows, :] = x1 + jnp.dot(att_ref[rows, :], wo_ref[...], preferred_element_type=F32)

    n_sub = tm // sub
    nxt = mix_out(0)
    for j in range(n_sub):
        cur = nxt
        if j + 1 < n_sub:
            nxt = mix_out(j + 1)
        attend(j, *cur)


def _mix_xattn(x, o, u, gv, ws, bs, wout, g, wq, qg, mk, mv, mem_spec, wo, tm, chunk, name):
    t = x.shape[0]
    row = lambda i: (i, 0)
    blk = pl.BlockSpec((tm, D_MODEL), row)
    half = pl.BlockSpec((tm, DA_WIDTH), row)
    return pl.pallas_call(
        functools.partial(_mix_xattn_body, chunk=chunk),
        grid=(t // tm,),
        in_specs=[blk, half, half, half, _const_spec((GM_GROUPS, chunk, chunk)),
                  _const_spec((GM_GROUPS, chunk, GM_CH)), _layer_spec(wout), _const_spec((1, D_MODEL)),
                  _layer_spec(wq), _const_spec((1, X_DH)), mem_spec, mem_spec, _layer_spec(wo)],
        out_specs=blk,
        out_shape=jax.ShapeDtypeStruct((t, D_MODEL), F32),
        scratch_shapes=[pltpu.VMEM((tm, D_MODEL), BF16), pltpu.VMEM((tm, D_MODEL), BF16)],
        compiler_params=_params(("parallel",)),
        name=name,
    )(x, o, u, gv, ws, bs, wout[0], g, wq[0], qg, mk, mv, wo[0])


def _conv_ffn_body(x_ref, g_ref, wup_ref, cw_ref, cb_ref, wdn_ref, hist_ref, xo_ref, cst_ref,
                   gp_ref, carry_ref, act_ref, *, nb, sub, tiles_per_b):
    i = pl.program_id(0)
    tm = x_ref.shape[0]
    n_sub = tm // sub
    r = sub // nb
    chained = tiles_per_b > 1 or n_sub > 1
    first_tile = (i % tiles_per_b) == 0
    n_chunks = D_FF // FF_CHUNK

    def gate(j):
        h = _rms(x_ref[j * sub:(j + 1) * sub, :], g_ref[...]).astype(BF16)
        for c in range(n_chunks):
            cs = slice(c * FF_CHUNK, (c + 1) * FF_CHUNK)
            g = jnp.dot(h, wup_ref[:, cs], preferred_element_type=F32)
            up = jnp.dot(h, wup_ref[:, D_FF + c * FF_CHUNK:D_FF + (c + 1) * FF_CHUNK],
                         preferred_element_type=F32)
            for bi in range(nb):
                lrows = slice(bi * r, (bi + 1) * r)
                rows = slice(j * sub + bi * r, j * sub + (bi + 1) * r)
                if not chained:
                    prev = hist_ref[bi, :, cs]
                elif j == 0:
                    prev = jnp.where(first_tile, hist_ref[bi, :, cs], carry_ref[:, cs])
                else:
                    prev = carry_ref[:, cs]
                gp = gp_ref.at[c]
                gp[6:8, :] = prev
                gp[8:8 + r, :] = g[lrows]
                conv = (cb_ref[:, cs] + cw_ref[0:1, cs] * gp[6:6 + r, :]
                        + cw_ref[1:2, cs] * gp[7:7 + r, :] + cw_ref[2:3, cs] * g[lrows])
                act = conv * (1.0 / (1.0 + jnp.exp(-conv))) * up[lrows]
                act_ref[rows, cs] = act.astype(BF16)
                last2 = gp[6 + r:8 + r, :]
                cst_ref[bi, :, cs] = last2
                if chained:
                    carry_ref[:, cs] = last2

    def down(j):
        rows = slice(j * sub, (j + 1) * sub)
        xo_ref[rows, :] = x_ref[rows, :] + jnp.dot(act_ref[rows, :], wdn_ref[...],
                                                   preferred_element_type=F32)

    gate(0)
    for j in range(n_sub):
        if j + 1 < n_sub:
            gate(j + 1)
        down(j)


def _conv_ffn(x, g, wup, cw, cb, wdn, hist, tm, sub, nb, tiles_per_b, name):
    t = x.shape[0]
    row = lambda i: (i, 0)
    blk = pl.BlockSpec((tm, D_MODEL), row)
    hist_spec = pl.BlockSpec((nb, CONV_W - 1, D_FF), lambda i: (i // tiles_per_b, 0, 0))
    return pl.pallas_call(
        functools.partial(_conv_ffn_body, nb=nb, sub=sub, tiles_per_b=tiles_per_b),
        grid=(t // tm,),
        in_specs=[blk, _const_spec((1, D_MODEL)), _layer_spec(wup, buffers=1),
                  _const_spec((CONV_W, D_FF)), _const_spec((1, D_FF)), _layer_spec(wdn, buffers=1),
                  hist_spec],
        out_specs=[blk, hist_spec],
        out_shape=[jax.ShapeDtypeStruct((t, D_MODEL), F32),
                   jax.ShapeDtypeStruct(hist.shape, F32)],
        scratch_shapes=[pltpu.VMEM((D_FF // FF_CHUNK, sub // nb + 8, FF_CHUNK), F32),
                        pltpu.VMEM((CONV_W - 1, D_FF), F32),
                        pltpu.VMEM((tm, D_FF), BF16)],
        compiler_params=_params(("arbitrary",)),
        name=name,
    )(x, g, wup[0], cw, cb, wdn[0], hist)


def _rope_angles(pos):
    inv = ROPE_THETA ** (-jnp.arange(ROT_HALF, dtype=F32) / ROT_HALF)
    ang = pos.astype(F32)[:, None] * inv[None, :]
    return jnp.cos(ang), jnp.sin(ang)


def _rope_tables(pos):
    cos, sin = _rope_angles(pos)
    n = pos.shape[0]
    ones = jnp.ones((n, DA_DK - ROT_DIM), F32)
    zeros_h = jnp.zeros((n, ROT_HALF), F32)
    zeros_r = jnp.zeros((n, DA_DK - ROT_DIM), F32)
    cos64 = jnp.concatenate([cos, cos, ones], axis=1)
    sa64 = jnp.concatenate([-sin, zeros_h, zeros_r], axis=1)
    sb64 = jnp.concatenate([zeros_h, sin, zeros_r], axis=1)
    two = lambda a: jnp.concatenate([a, a], axis=1)
    return two(cos64), two(sa64), two(sb64)


def _post_attention(x, o, u, gv, w, mk, mv, mem_spec, hist, seq, tm, tag):
    t = x.shape[0]
    row2 = lambda a: a.reshape(1, -1)
    chunk = min(seq, GM_CHUNK)
    ws = w['gm_w_s'][:, :chunk, :chunk]
    bs = jnp.broadcast_to(w['gm_b'][:, :chunk, None], (GM_GROUPS, chunk, GM_CH))
    x = _mix_xattn(x, o.reshape(t, DA_WIDTH), u, gv, ws, bs, w['w_out'], row2(w['norm_x_g']), w['wq_c'],
                   row2(w['xq_norm_g']), mk, mv, mem_spec, w['wo_c'], min(tm, seq), chunk,
                   f"mix_xattn_{tag}")
    sub = min(SUB_ROWS, tm)
    return _conv_ffn(x, row2(w['norm_ffn_g']), w['w_up'], w['conv_w'], row2(w['conv_b']),
                     w['w_down'], hist, tm, sub, max(1, sub // seq), max(1, seq // tm), f"conv_ffn_{tag}")


def kernel(x_prompt, x_sample, cache_da_k, cache_da_v, cache_mem_k, cache_mem_v, state_ffn_conv, mem_prompt, norm_mix_g, w_in, da_q_norm_g, da_k_norm_g, lambda_q1, lambda_k1, lambda_q2, lambda_k2, da_subln_g, gm_norm_g, gm_w_s, gm_b, w_out, norm_x_g, norm_mem_g, wq_c, wk_c, wv_c, wo_c, xq_norm_g, xk_norm_g, norm_ffn_g, w_up, conv_w, conv_b, w_down):
    bp, sp, _ = x_prompt.shape
    bs_, ss, _ = x_sample.shape
    depth = w_in.shape[0]
    past = cache_da_k.shape[2]
    tm_w = min(WIDE_TILE, sp)
    tm_s = bs_ * ss
    tm_mem = min(TOKEN_TILE, bp * MEM_LEN)
    assert sp % tm_w == 0 and tm_w % SUB_ROWS == 0 and sp % Q_TILE == 0 and SUB_ROWS % GM_CHUNK == 0
    assert ss <= GM_CHUNK and tm_s % SUBLANES == 0 and ss >= CONV_W - 1 and ss % SUBLANES == 0
    assert (bp * MEM_LEN) % tm_mem == 0 and tm_mem % MEM_LEN == 0

    pos_p = jnp.arange(sp)
    rope_p = _rope_tables(pos_p)
    cos_p, sin_p = _rope_angles(pos_p)
    rope_kp = (cos_p.T, sin_p.T)
    rope_s = tuple(jnp.tile(a, (bs_, 1)) for a in _rope_tables(past + jnp.arange(ss)))
    cache_kt = jnp.transpose(cache_da_k, (0, 1, 3, 4, 5, 2)).reshape(depth, bs_, DA_QK_COLS, past)
    cache_v2 = cache_da_v.reshape(depth * bs_ * past * DA_HEADS, DA_DV)
    hist_p = jnp.zeros((bp, CONV_W - 1, D_FF), F32)
    mem_flat = mem_prompt.reshape(bp * MEM_LEN, D_MODEL)

    kt_all = lax.empty((depth, bp, DA_QK_COLS, sp), F32)
    v_all = lax.empty((depth * bp * sp * DA_HEADS, DA_DV), F32)
    mk_all = lax.empty((depth * bp * MEM_ROWS, LANES), F32)
    mv_all = lax.empty((depth * bp * MEM_ROWS, LANES), F32)
    cmk, cmv = _mem_rows_view(cache_mem_k), _mem_rows_view(cache_mem_v)

    xp = x_prompt.reshape(bp * sp, D_MODEL)
    xs = x_sample.reshape(tm_s, D_MODEL)
    outs = {n: [] for n in ('fc_p', 'dk_s', 'dv_s', 'gv_s', 'fc_s')}
    row2 = lambda a: a.reshape(1, -1)
    wb = {n: a.astype(BF16) for n, a in dict(w_in=w_in, w_out=w_out, wq_c=wq_c, wk_c=wk_c, wv_c=wv_c,
                                             wo_c=wo_c, w_up=w_up, w_down=w_down).items()}
    for l in range(depth):
        lam_init = 0.8 - 0.6 * math.exp(-0.3 * l)
        w = dict(gm_w_s=gm_w_s[l], gm_b=gm_b[l], w_out=(wb['w_out'], l),
                 norm_x_g=norm_x_g[l], wq_c=(wb['wq_c'], l), wo_c=(wb['wo_c'], l),
                 xq_norm_g=xq_norm_g[l], norm_ffn_g=norm_ffn_g[l], w_up=(wb['w_up'], l),
                 conv_w=conv_w[l], conv_b=conv_b[l], w_down=(wb['w_down'], l))
        w_in_l = (wb['w_in'], l)
        qg = jnp.tile(da_q_norm_g[l], 2).reshape(1, -1)
        kg = jnp.tile(da_k_norm_g[l], 2).reshape(1, -1)
        kg_col = jnp.broadcast_to(da_k_norm_g[l][:, None], (DA_DK, SUB_ROWS))
        gmg = jnp.tile(gm_norm_g[l], GM_GROUPS).reshape(1, -1)
        lams = [row2(a[l]) for a in (lambda_q1, lambda_k1, lambda_q2, lambda_k2)]
        subg = row2(da_subln_g[l])
        g_mix = row2(norm_mix_g[l])

        mk_all, mv_all, mkb, mvb = _mem_kv(mem_flat, row2(norm_mem_g[l]), (wb['wk_c'], l),
                                           (wb['wv_c'], l), row2(xk_norm_g[l]), mk_all, mv_all, l,
                                           tm_mem, f"mem_kv_{l}")
        q, kt_all, v_all, vb, u, gv = _mix_in_prompt(xp, g_mix, w_in_l, qg, kg_col, gmg, rope_p, rope_kp,
                                                     kt_all, v_all, l, sp, tm_w, f"mix_in_p{l}")
        o = _da_prompt(q.reshape(bp, sp, DA_QK_COLS), kt_all, vb.reshape(bp, sp, DA_WIDTH), l, lams, subg,
                       lam_init, f"da_p{l}")
        mem_spec_p = pl.BlockSpec((MEM_LEN, D_MODEL), lambda i: (i // (sp // tm_w), 0))
        xp, cst = _post_attention(xp, o, u, gv, w, mkb, mvb, mem_spec_p, hist_p, sp, tm_w, f"p{l}")
        outs['fc_p'].append(cst)

        q, k, v, u, gv = _mix_in_sample(xs, g_mix, w_in_l, qg, kg, gmg, rope_s, f"mix_in_s{l}")
        o = _da_sample(q.reshape(bs_, ss, DA_QK_COLS), k.reshape(bs_, ss, DA_QK_COLS),
                       v.reshape(bs_, ss, DA_WIDTH), cache_kt, cache_v2, l, lams, subg, lam_init,
                       f"da_s{l}")
        mem_spec_s = pl.BlockSpec((MEM_ROWS, LANES), lambda i, l=l: (l * bs_ + i, 0))
        xs, cst = _post_attention(xs, o, u, gv, w, cmk, cmv, mem_spec_s, state_ffn_conv[l],
                                  ss, tm_s, f"s{l}")
        outs['dk_s'].append(k.reshape(bs_, ss, DA_HEADS, 2, DA_DK))
        outs['dv_s'].append(v.reshape(bs_, ss, DA_HEADS, DA_DV))
        outs['gv_s'].append(gv.reshape(bs_, ss, GM_GROUPS, GM_CH))
        outs['fc_s'].append(cst)

    st = {n: jnp.stack(a) for n, a in outs.items()}
    dk_p = jnp.transpose(kt_all.reshape(depth, bp, DA_HEADS, 2, DA_DK, sp), (0, 1, 5, 2, 3, 4))
    dv_p = v_all.reshape(depth, bp, sp, DA_HEADS, DA_DV)
    mk_p, mv_p = (_mem_rows_unview(a, (depth, bp)) for a in (mk_all, mv_all))
    return (xp.reshape(bp, sp, D_MODEL), xs.reshape(bs_, ss, D_MODEL), dk_p, dv_p, mk_p, mv_p,
            st['fc_p'], st['dk_s'], st['dv_s'], st['gv_s'], st['fc_s'])
```

```python
import functools
import math

import numpy as np
import jax
import jax.numpy as jnp
from jax import lax
from jax.experimental import pallas as pl
from jax.experimental.pallas import tpu as pltpu

F32 = jnp.float32
BF16 = jnp.bfloat16

D_MODEL = 1024
CHUNK = 64
DA_HEADS = 4
DA_DK = 64
DA_DV = 128
DA_QK_COLS = DA_HEADS * 2 * DA_DK
DA_WIDTH = DA_HEADS * DA_DV
ROT_DIM = 16
ROT_HALF = ROT_DIM // 2
ROPE_THETA = 500000.0
GM_GROUPS = 4
GM_CH = 128
GM_WIDTH = GM_GROUPS * GM_CH
GM_CHUNK = 128
MEM_LEN = 256
X_HEADS = 4
X_DH = 256
D_FF = 2816
CONV_W = 3
EPS = 1e-6
IN_COLS = 2 * DA_QK_COLS + DA_WIDTH + 2 * GM_WIDTH
NEG = float(np.finfo(np.float32).min)
Q_SCALE = (DA_DK ** -0.5) * math.log2(math.e)

LANES = 128
SUBLANES = 8
VMEM_LIMIT_BYTES = 56 * 1024 * 1024
FF_CHUNK = 256
TOKEN_TILE = 512
WIDE_TILE = 1024
Q_TILE = 256
DA_HEADS_PER_STEP = 2
SUB_ROWS = 512

_NT = (((1,), (1,)), ((), ()))

MEM_ROWS = MEM_LEN * (X_DH // LANES) * X_HEADS


def _mem_rows_view(a):
    lead = a.shape[:-3]
    a = a.reshape(*lead, MEM_LEN, X_HEADS, X_DH // LANES, LANES)
    return jnp.swapaxes(a, -3, -2).reshape(-1, LANES)


def _mem_rows_unview(a, lead):
    a = a.reshape(*lead, MEM_LEN, X_DH // LANES, X_HEADS, LANES)
    return jnp.swapaxes(a, -3, -2).reshape(*lead, MEM_LEN, X_HEADS, X_DH)


def _mem_head_rows(h, c, base=0):
    return pl.ds(base + c * X_HEADS + h, MEM_LEN, stride=(X_DH // LANES) * X_HEADS)


def _params(sem):
    return pltpu.CompilerParams(dimension_semantics=sem, vmem_limit_bytes=VMEM_LIMIT_BYTES)


def _const_spec(shape):
    n = len(shape)
    return pl.BlockSpec(shape, lambda *_: (0,) * n)


_ANY = pl.BlockSpec(memory_space=pl.ANY)


def _layer_spec(wl, buffers=None):
    w, l = wl
    kw = {} if buffers is None else dict(pipeline_mode=pl.Buffered(buffers))
    return pl.BlockSpec((None,) + w.shape[1:], lambda *_: (l, 0, 0), **kw)


def _rms(x, g):
    return (x * lax.rsqrt(jnp.mean(x * x, axis=-1, keepdims=True) + EPS)) * g


def _gelu(x):
    return x * (0.5 * (1.0 + jnp.tanh(0.7978845608028654 * (x + 0.044715 * (x * x * x)))))


def _qk_slab(t, gain, cosv, sa, sb):
    lo = lax.broadcasted_iota(jnp.int32, (1, LANES), 1) < DA_DK
    sq = t * t
    s_lo = jnp.sum(jnp.where(lo, sq, 0.0), axis=-1, keepdims=True)
    s_hi = jnp.sum(jnp.where(lo, 0.0, sq), axis=-1, keepdims=True)
    ms = jnp.where(lo, s_lo, s_hi) * (1.0 / DA_DK)
    y = (t * lax.rsqrt(ms + EPS)) * gain
    return y * cosv + pltpu.roll(y, LANES - ROT_HALF, 1) * sa + pltpu.roll(y, ROT_HALF, 1) * sb


def _gmlp_cols(z, c0, gmg_ref, u_ref, gv_ref, rows=slice(None)):
    u_ref[rows, :] = _gelu(z[:, c0:c0 + GM_WIDTH]).astype(u_ref.dtype)
    c1 = c0 + GM_WIDTH
    for s in range(GM_GROUPS):
        cols = slice(s * GM_CH, (s + 1) * GM_CH)
        t = _gelu(z[:, c1 + s * GM_CH:c1 + (s + 1) * GM_CH])
        gv_ref[rows, cols] = _rms(t, gmg_ref[:, cols]).astype(gv_ref.dtype)


def _mix_in_sample_body(x_ref, g_ref, w_ref, qg_ref, kg_ref, gmg_ref, cos_ref, sa_ref, sb_ref,
                        q_ref, k_ref, v_ref, u_ref, gv_ref):
    h = _rms(x_ref[...], g_ref[...]).astype(BF16)
    z = jnp.dot(h, w_ref[...], preferred_element_type=F32)
    cosv, sa, sb = cos_ref[...], sa_ref[...], sb_ref[...]
    for s in range(DA_HEADS):
        cols = slice(s * LANES, (s + 1) * LANES)
        q_ref[:, cols] = (_qk_slab(z[:, cols], qg_ref[...], cosv, sa, sb) * Q_SCALE).astype(q_ref.dtype)
        kc = slice(DA_QK_COLS + s * LANES, DA_QK_COLS + (s + 1) * LANES)
        k_ref[:, cols] = _qk_slab(z[:, kc], kg_ref[...], cosv, sa, sb)
    c2 = 2 * DA_QK_COLS
    v_ref[...] = z[:, c2:c2 + DA_WIDTH]
    _gmlp_cols(z, c2 + DA_WIDTH, gmg_ref, u_ref, gv_ref)


def _mix_in_sample(x, g, w, qg, kg, gmg, rope, name):
    t = x.shape[0]
    row = pl.BlockSpec((t, DA_QK_COLS), lambda i: (0, 0))
    rope_spec = _const_spec((t, LANES))
    return pl.pallas_call(
        _mix_in_sample_body,
        grid=(1,),
        in_specs=[_const_spec((t, D_MODEL)), _const_spec((1, D_MODEL)), _layer_spec(w),
                  _const_spec((1, LANES)), _const_spec((1, LANES)), _const_spec((1, GM_WIDTH)),
                  rope_spec, rope_spec, rope_spec],
        out_specs=[row] * 5,
        out_shape=[jax.ShapeDtypeStruct((t, DA_QK_COLS), BF16),
                   jax.ShapeDtypeStruct((t, DA_QK_COLS), F32),
                   jax.ShapeDtypeStruct((t, DA_WIDTH), F32),
                   jax.ShapeDtypeStruct((t, GM_WIDTH), BF16),
                   jax.ShapeDtypeStruct((t, GM_WIDTH), F32)],
        compiler_params=_params(("arbitrary",)),
        name=name,
    )(x, g, w[0], qg, kg, gmg, *rope)


def _mix_in_prompt_body(x_ref, g_ref, w_ref, qg_ref, kg_ref, gmg_ref, cos_ref, sa_ref, sb_ref,
                        cosk_ref, sink_ref, kt_in, v_in, q_ref, kt_ref, v_ref, vb_ref, u_ref, gv_ref):
    del kt_in, v_in
    tm = x_ref.shape[0]
    c_k, c_v = DA_QK_COLS, 2 * DA_QK_COLS

    def project(j):
        rows = slice(j * SUB_ROWS, (j + 1) * SUB_ROWS)
        h = _rms(x_ref[rows, :], g_ref[...]).astype(BF16)
        return jnp.dot(h, w_ref[...], preferred_element_type=F32)

    def finish(j, z):
        rows = slice(j * SUB_ROWS, (j + 1) * SUB_ROWS)
        cosv, sa, sb = cos_ref[rows, :], sa_ref[rows, :], sb_ref[rows, :]
        for s in range(DA_HEADS):
            cols = slice(s * LANES, (s + 1) * LANES)
            q_ref[rows, cols] = (_qk_slab(z[:, cols], qg_ref[...], cosv, sa, sb) * Q_SCALE).astype(q_ref.dtype)
            v_ref[pl.ds(j * SUB_ROWS * DA_HEADS + s, SUB_ROWS, stride=DA_HEADS), :] = (
                z[:, c_v + s * DA_DV:c_v + (s + 1) * DA_DV])
        vb_ref[rows, :] = z[:, c_v:c_v + DA_WIDTH].astype(vb_ref.dtype)
        zk = z[:, c_k:c_k + DA_QK_COLS].T
        ck, sk, kg = cosk_ref[:, rows], sink_ref[:, rows], kg_ref[...]
        for grp in range(2 * DA_HEADS):
            r0 = grp * DA_DK
            t = zk[r0:r0 + DA_DK, :]
            inv = lax.rsqrt(jnp.sum(t * t, axis=0, keepdims=True) * (1.0 / DA_DK) + EPS)
            y = (t * inv) * kg
            ya, yb = y[:ROT_HALF], y[ROT_HALF:ROT_DIM]
            kt_ref[r0:r0 + ROT_HALF, rows] = ya * ck - yb * sk
            kt_ref[r0 + ROT_HALF:r0 + ROT_DIM, rows] = yb * ck + ya * sk
            kt_ref[r0 + ROT_DIM:r0 + DA_DK, rows] = y[ROT_DIM:]
        _gmlp_cols(z, c_v + DA_WIDTH, gmg_ref, u_ref, gv_ref, rows)

    n_sub = tm // SUB_ROWS
    nxt = project(0)
    for j in range(n_sub):
        cur = nxt
        if j + 1 < n_sub:
            nxt = project(j + 1)
        finish(j, cur)


def _mix_in_prompt(x, g, w, qg, kg_col, gmg, rope, rope_k, kt_all, v_all, layer, seq, tm, name):
    t = x.shape[0]
    per_b = seq // tm
    row = lambda i: (i, 0)
    rope_spec = pl.BlockSpec((tm, LANES), lambda i: (i % per_b, 0))
    ropek_spec = pl.BlockSpec((SUBLANES, tm), lambda i: (0, i % per_b))
    half = pl.BlockSpec((tm, DA_QK_COLS), row)
    kt_spec = pl.BlockSpec((None, None, DA_QK_COLS, tm), lambda i: (layer, i // per_b, 0, i % per_b))
    v_spec = pl.BlockSpec((tm * DA_HEADS, DA_DV), lambda i: (layer * (t // tm) + i, 0))
    return pl.pallas_call(
        _mix_in_prompt_body,
        grid=(t // tm,),
        in_specs=[pl.BlockSpec((tm, D_MODEL), row), _const_spec((1, D_MODEL)),
                  _layer_spec(w),
                  _const_spec((1, LANES)), _const_spec((DA_DK, SUB_ROWS)), _const_spec((1, GM_WIDTH)),
                  rope_spec, rope_spec, rope_spec, ropek_spec, ropek_spec, _ANY, _ANY],
        out_specs=[half, kt_spec, v_spec, half, half, half],
        out_shape=[jax.ShapeDtypeStruct((t, DA_QK_COLS), BF16),
                   jax.ShapeDtypeStruct(kt_all.shape, F32),
                   jax.ShapeDtypeStruct(v_all.shape, F32),
                   jax.ShapeDtypeStruct((t, DA_WIDTH), BF16),
                   jax.ShapeDtypeStruct((t, GM_WIDTH), BF16),
                   jax.ShapeDtypeStruct((t, GM_WIDTH), BF16)],
        input_output_aliases={11: 1, 12: 2},
        compiler_params=_params(("parallel",)),
        name=name,
    )(x, g, w[0], qg, kg_col, gmg, *rope, *rope_k, kt_all, v_all)


def _lambda(lq1, lk1, lq2, lk2, lam_init):
    a = jnp.sum(lq1[...] * lk1[...], axis=-1, keepdims=True)
    b = jnp.sum(lq2[...] * lk2[...], axis=-1, keepdims=True)
    return jnp.exp(a) - jnp.exp(b) + lam_init


def _subln(o, g, lam_init):
    return _rms(o, g) * (1.0 - lam_init)


def _split_maps(q):
    lo = lax.broadcasted_iota(jnp.int32, (1, LANES), 1) < DA_DK
    zero = jnp.zeros_like(q)
    return jnp.where(lo, q, zero), jnp.where(lo, zero, q)


def _da_prompt_body(lq1, lk1, lq2, lk2, subg_ref, q_ref, kt_ref, v_ref, o_ref, *, lam_init, bq):
    seq = q_ref.shape[1]
    lam = _lambda(lq1, lk1, lq2, lk2, lam_init)
    mask = ((lax.broadcasted_iota(jnp.int32, (bq, bq), 1) // CHUNK)
            <= (lax.broadcasted_iota(jnp.int32, (bq, bq), 0) // CHUNK))
    heads = []
    for h in range(DA_HEADS_PER_STEP):
        cols = slice(h * LANES, (h + 1) * LANES)
        heads.append((_split_maps(q_ref[0, :, cols]),
                      kt_ref[cols, :].astype(BF16),
                      jnp.concatenate([v_ref[0, :, cols], jnp.ones((seq, DA_DV), BF16)], axis=1)))

    def scores(h, qi):
        qm, kt, _ = heads[h]
        lo, hi = qi * bq, (qi + 1) * bq
        out = []
        for m in range(2):
            q = qm[m][lo:hi]
            sd = jnp.dot(q, kt[:, lo:hi], preferred_element_type=F32)
            so = jnp.dot(q, kt[:, :lo], preferred_element_type=F32) if qi > 0 else None
            out.append((sd, so))
        return out

    def attend(h, qi, cur):
        va = heads[h][2]
        lo, hi = qi * bq, (qi + 1) * bq
        outs = []
        for m in range(2):
            sd, so = cur[m]
            sd = jnp.where(mask, sd, NEG)
            mx = jnp.max(sd, axis=-1, keepdims=True)
            if qi > 0:
                mx = jnp.maximum(mx, jnp.max(so, axis=-1, keepdims=True))
            ov = jnp.dot(jnp.exp2(sd - mx).astype(BF16), va[lo:hi], preferred_element_type=F32)
            if qi > 0:
                ov = ov + jnp.dot(jnp.exp2(so - mx).astype(BF16), va[:lo], preferred_element_type=F32)
            outs.append(ov[:, :DA_DV] * (1.0 / ov[:, DA_DV:DA_DV + 1]))
        o = outs[0] - lam * outs[1]
        o_ref[0, lo:hi, h * LANES:(h + 1) * LANES] = _subln(o, subg_ref[...], lam_init).astype(o_ref.dtype)

    nq = seq // bq
    nxt = [scores(h, 0) for h in range(DA_HEADS_PER_STEP)]
    for qi in range(nq):
        cur = nxt
        if qi + 1 < nq:
            nxt = [scores(h, qi + 1) for h in range(DA_HEADS_PER_STEP)]
        for h in range(DA_HEADS_PER_STEP):
            attend(h, qi, cur[h])


def _da_prompt(q, kt_all, vb, layer, lams, subg, lam_init, name):
    b, s, _ = q.shape
    width = DA_HEADS_PER_STEP * LANES
    lam_spec = _const_spec((1, DA_DK))
    q_spec = pl.BlockSpec((1, s, width), lambda bi, h: (bi, 0, h))
    kt_spec = pl.BlockSpec((None, None, width, s), lambda bi, h: (layer, bi, h, 0))
    return pl.pallas_call(
        functools.partial(_da_prompt_body, lam_init=lam_init, bq=Q_TILE),
        grid=(b, DA_HEADS // DA_HEADS_PER_STEP),
        in_specs=[lam_spec, lam_spec, lam_spec, lam_spec, _const_spec((1, DA_DV)),
                  q_spec, kt_spec, q_spec],
        out_specs=q_spec,
        out_shape=jax.ShapeDtypeStruct((b, s, DA_WIDTH), BF16),
        compiler_params=_params(("parallel", "parallel")),
        name=name,
    )(*lams, subg, q, kt_all, vb)


def _da_sample_body(lq1, lk1, lq2, lk2, subg_ref, q_ref, ktp_ref, kn_ref, vp_ref, vn_ref, o_ref,
                    *, lam_init, past, sq):
    lam = _lambda(lq1, lk1, lq2, lk2, lam_init)
    q_chunk = (past + lax.broadcasted_iota(jnp.int32, (sq, 1), 0)) // CHUNK
    mask_p = (lax.broadcasted_iota(jnp.int32, (1, past), 1) // CHUNK) <= q_chunk
    mask_n = ((past + lax.broadcasted_iota(jnp.int32, (1, sq), 1)) // CHUNK) <= q_chunk
    for h in range(DA_HEADS):
        cols = slice(h * LANES, (h + 1) * LANES)
        qm = _split_maps(q_ref[0, :, cols])
        ktp = ktp_ref[cols, :].astype(BF16)
        kn = kn_ref[0, :, cols].astype(BF16)
        probs = []
        for m in range(2):
            sp = jnp.where(mask_p, jnp.dot(qm[m], ktp, preferred_element_type=F32), NEG)
            sn = jnp.where(mask_n, lax.dot_general(qm[m], kn, _NT, preferred_element_type=F32), NEG)
            mx = jnp.maximum(jnp.max(sp, axis=-1, keepdims=True), jnp.max(sn, axis=-1, keepdims=True))
            pp = jnp.exp2(sp - mx)
            pn = jnp.exp2(sn - mx)
            inv = 1.0 / (jnp.sum(pp, axis=-1, keepdims=True) + jnp.sum(pn, axis=-1, keepdims=True))
            probs.append((pp * inv, pn * inv))
        ap = (probs[0][0] - lam * probs[1][0]).astype(BF16)
        an = (probs[0][1] - lam * probs[1][1]).astype(BF16)
        vp = vp_ref[pl.ds(h, past, stride=DA_HEADS), :]
        o = (jnp.dot(ap, vp.astype(BF16), preferred_element_type=F32)
             + jnp.dot(an, vn_ref[0, :, cols].astype(BF16), preferred_element_type=F32))
        o_ref[0, :, cols] = _subln(o, subg_ref[...], lam_init).astype(o_ref.dtype)


def _da_sample(q, k_new, v_new, cache_kt, cache_v, layer, lams, subg, lam_init, name):
    b, sq, _ = q.shape
    past = cache_kt.shape[3]
    lam_spec = _const_spec((1, DA_DK))
    new_spec = pl.BlockSpec((1, sq, DA_QK_COLS), lambda bi: (bi, 0, 0))
    ktp_spec = pl.BlockSpec((None, None, DA_QK_COLS, past), lambda bi: (layer, bi, 0, 0))
    vp_spec = pl.BlockSpec((past * DA_HEADS, DA_DV), lambda bi: (layer * b + bi, 0))
    return pl.pallas_call(
        functools.partial(_da_sample_body, lam_init=lam_init, past=past, sq=sq),
        grid=(b,),
        in_specs=[lam_spec, lam_spec, lam_spec, lam_spec, _const_spec((1, DA_DV)),
                  new_spec, ktp_spec, new_spec, vp_spec, new_spec],
        out_specs=new_spec,
        out_shape=jax.ShapeDtypeStruct((b, sq, DA_WIDTH), BF16),
        compiler_params=_params(("parallel",)),
        name=name,
    )(*lams, subg, q, cache_kt, k_new, cache_v, v_new)


def _mem_kv_body(mem_ref, g_ref, wk_ref, wv_ref, kg_ref, k_ref, v_ref, kb_ref, vb_ref, *, nb):
    m = _rms(mem_ref[...], g_ref[...]).astype(BF16)
    k = jnp.dot(m, wk_ref[...], preferred_element_type=F32)
    v = jnp.dot(m, wv_ref[...], preferred_element_type=F32)
    vb_ref[...] = v.astype(vb_ref.dtype)
    for h in range(X_HEADS):
        cols = slice(h * X_DH, (h + 1) * X_DH)
        kn = _rms(k[:, cols], kg_ref[...])
        kb_ref[:, cols] = kn.astype(kb_ref.dtype)
        for bi in range(nb):
            rows = slice(bi * MEM_LEN, (bi + 1) * MEM_LEN)
            for c in range(X_DH // LANES):
                half = slice(c * LANES, (c + 1) * LANES)
                k_ref[_mem_head_rows(h, c, bi * MEM_ROWS), :] = kn[rows, half]
                v_ref[_mem_head_rows(h, c, bi * MEM_ROWS), :] = v[rows, cols][:, half]


def _mem_kv(mem, g, wk, wv, kg, tm, name):
    t = mem.shape[0]
    depth = wk.shape[0]
    nb = tm // MEM_LEN
    n_t = t // tm
    layer_mat = lambda a: pl.BlockSpec((None,) + a.shape[1:], lambda l, i: (l, 0, 0))
    out_spec = pl.BlockSpec((nb * MEM_ROWS, LANES), lambda l, i: (l * n_t + i, 0))
    copy_spec = pl.BlockSpec((None, tm, D_MODEL), lambda l, i: (l, i, 0))
    rows = depth * (t // MEM_LEN) * MEM_ROWS
    return pl.pallas_call(
        functools.partial(_mem_kv_body, nb=nb),
        grid=(depth, n_t),
        in_specs=[pl.BlockSpec((tm, D_MODEL), lambda l, i: (i, 0)), layer_mat(g), layer_mat(wk),
                  layer_mat(wv), layer_mat(kg)],
        out_specs=[out_spec, out_spec, copy_spec, copy_spec],
        out_shape=[jax.ShapeDtypeStruct((rows, LANES), F32), jax.ShapeDtypeStruct((rows, LANES), F32),
                   jax.ShapeDtypeStruct((depth, t, D_MODEL), BF16),
                   jax.ShapeDtypeStruct((depth, t, D_MODEL), BF16)],
        compiler_params=_params(("parallel", "parallel")),
        name=name,
    )(mem, g, wk, wv, kg)


def _mix_xattn_body(x_ref, o_ref, u_ref, gv_ref, ws_ref, bs_ref, wout_ref, g_ref, wq_ref, qg_ref,
                    mk_ref, mv_ref, wo_ref, xo_ref, cat_ref, att_ref, *, chunk):
    tm = x_ref.shape[0]
    sub = min(SUB_ROWS, tm)
    flat = mk_ref.shape[-1] == D_MODEL
    tri = (lax.broadcasted_iota(jnp.int32, (chunk, chunk), 1)
           <= lax.broadcasted_iota(jnp.int32, (chunk, chunk), 0))
    ws = [jnp.where(tri, ws_ref[g], 0.0).astype(BF16) for g in range(GM_GROUPS)]

    def mix_out(j):
        r0 = j * sub
        rows = slice(r0, r0 + sub)
        cat_ref[rows, :DA_WIDTH] = o_ref[rows, :]
        for g in range(GM_GROUPS):
            cols = slice(g * GM_CH, (g + 1) * GM_CH)
            bias = bs_ref[g]
            for c in range(sub // chunk):
                cr = slice(r0 + c * chunk, r0 + (c + 1) * chunk)
                s = jnp.dot(ws[g], gv_ref[cr, cols].astype(BF16), preferred_element_type=F32) + bias
                cat_ref[cr, DA_WIDTH + g * GM_CH:DA_WIDTH + (g + 1) * GM_CH] = (
                    u_ref[cr, cols].astype(F32) * s).astype(BF16)
        x1 = x_ref[rows, :] + jnp.dot(cat_ref[rows, :], wout_ref[...], preferred_element_type=F32)
        h = _rms(x1, g_ref[...]).astype(BF16)
        return x1, jnp.dot(h, wq_ref[...], preferred_element_type=F32)

    def attend(j, x1, q):
        rows = slice(j * sub, (j + 1) * sub)
        for hh in range(X_HEADS):
            cols = slice(hh * X_DH, (hh + 1) * X_DH)
            if flat:
                mk, mv = mk_ref[:, cols], mv_ref[:, cols]
            else:
                halves = range(X_DH // LANES)
                mk = jnp.concatenate([mk_ref[_mem_head_rows(hh, c), :] for c in halves], axis=1)
                mv = jnp.concatenate([mv_ref[_mem_head_rows(hh, c), :] for c in halves], axis=1)
            qn = (_rms(q[:, cols], qg_ref[...]) * (X_DH ** -0.5)).astype(BF16)
            s = lax.dot_general(qn, mk.astype(BF16), _NT, preferred_element_type=F32)
            p = jnp.exp(s - jnp.max(s, axis=-1, keepdims=True))
            p = p * (1.0 / jnp.sum(p, axis=-1, keepdims=True))
            att_ref[rows, cols] = jnp.dot(p.astype(BF16), mv.astype(BF16),
                                          preferred_element_type=F32).astype(BF16)
        xo_ref[rows, :] = x1 + jnp.dot(att_ref[rows, :], wo_ref[...], preferred_element_type=F32)

    n_sub = tm // sub
    nxt = mix_out(0)
    for j in range(n_sub):
        cur = nxt
        if j + 1 < n_sub:
            nxt = mix_out(j + 1)
        attend(j, *cur)


def _mix_xattn(x, o, u, gv, ws, bs, wout, g, wq, qg, mk, mv, mem_spec, wo, tm, chunk, name):
    t = x.shape[0]
    row = lambda i: (i, 0)
    blk = pl.BlockSpec((tm, D_MODEL), row)
    half = pl.BlockSpec((tm, DA_WIDTH), row)
    return pl.pallas_call(
        functools.partial(_mix_xattn_body, chunk=chunk),
        grid=(t // tm,),
        in_specs=[blk, half, half, half, _const_spec((GM_GROUPS, chunk, chunk)),
                  _const_spec((GM_GROUPS, chunk, GM_CH)), _layer_spec(wout), _const_spec((1, D_MODEL)),
                  _layer_spec(wq), _const_spec((1, X_DH)), mem_spec, mem_spec, _layer_spec(wo)],
        out_specs=blk,
        out_shape=jax.ShapeDtypeStruct((t, D_MODEL), F32),
        scratch_shapes=[pltpu.VMEM((tm, D_MODEL), BF16), pltpu.VMEM((tm, D_MODEL), BF16)],
        compiler_params=_params(("parallel",)),
        name=name,
    )(x, o, u, gv, ws, bs, wout[0], g, wq[0], qg, mk, mv, wo[0])


def _conv_ffn_body(x_ref, g_ref, wup_ref, cw_ref, cb_ref, wdn_ref, hist_ref, xo_ref, cst_ref,
                   gp_ref, carry_ref, act_ref, *, nb, sub, tiles_per_b):
    i = pl.program_id(0)
    tm = x_ref.shape[0]
    n_sub = tm // sub
    r = sub // nb
    chained = tiles_per_b > 1 or n_sub > 1
    first_tile = (i % tiles_per_b) == 0
    n_chunks = D_FF // FF_CHUNK

    def gate(j):
        h = _rms(x_ref[j * sub:(j + 1) * sub, :], g_ref[...]).astype(BF16)
        for c in range(n_chunks):
            cs = slice(c * FF_CHUNK, (c + 1) * FF_CHUNK)
            g = jnp.dot(h, wup_ref[:, cs], preferred_element_type=F32)
            up = jnp.dot(h, wup_ref[:, D_FF + c * FF_CHUNK:D_FF + (c + 1) * FF_CHUNK],
                         preferred_element_type=F32)
            for bi in range(nb):
                lrows = slice(bi * r, (bi + 1) * r)
                rows = slice(j * sub + bi * r, j * sub + (bi + 1) * r)
                if not chained:
                    prev = hist_ref[bi, :, cs]
                elif j == 0:
                    prev = jnp.where(first_tile, hist_ref[bi, :, cs], carry_ref[:, cs])
                else:
                    prev = carry_ref[:, cs]
                gp = gp_ref.at[c]
                gp[6:8, :] = prev
                gp[8:8 + r, :] = g[lrows]
                conv = (cb_ref[:, cs] + cw_ref[0:1, cs] * gp[6:6 + r, :]
                        + cw_ref[1:2, cs] * gp[7:7 + r, :] + cw_ref[2:3, cs] * g[lrows])
                act = conv * (1.0 / (1.0 + jnp.exp(-conv))) * up[lrows]
                act_ref[rows, cs] = act.astype(BF16)
                last2 = gp[6 + r:8 + r, :]
                cst_ref[bi, :, cs] = last2
                if chained:
                    carry_ref[:, cs] = last2

    def down(j):
        rows = slice(j * sub, (j + 1) * sub)
        xo_ref[rows, :] = x_ref[rows, :] + jnp.dot(act_ref[rows, :], wdn_ref[...],
                                                   preferred_element_type=F32)

    gate(0)
    for j in range(n_sub):
        if j + 1 < n_sub:
            gate(j + 1)
        down(j)


def _conv_ffn(x, g, wup, cw, cb, wdn, hist, tm, sub, nb, tiles_per_b, name):
    t = x.shape[0]
    row = lambda i: (i, 0)
    blk = pl.BlockSpec((tm, D_MODEL), row)
    hist_spec = pl.BlockSpec((nb, CONV_W - 1, D_FF), lambda i: (i // tiles_per_b, 0, 0))
    return pl.pallas_call(
        functools.partial(_conv_ffn_body, nb=nb, sub=sub, tiles_per_b=tiles_per_b),
        grid=(t // tm,),
        in_specs=[blk, _const_spec((1, D_MODEL)), _layer_spec(wup, buffers=1),
                  _const_spec((CONV_W, D_FF)), _const_spec((1, D_FF)), _layer_spec(wdn, buffers=1),
                  hist_spec],
        out_specs=[blk, hist_spec],
        out_shape=[jax.ShapeDtypeStruct((t, D_MODEL), F32),
                   jax.ShapeDtypeStruct(hist.shape, F32)],
        scratch_shapes=[pltpu.VMEM((D_FF // FF_CHUNK, sub // nb + 8, FF_CHUNK), F32),
                        pltpu.VMEM((CONV_W - 1, D_FF), F32),
                        pltpu.VMEM((tm, D_FF), BF16)],
        compiler_params=_params(("arbitrary",)),
        name=name,
    )(x, g, wup[0], cw, cb, wdn[0], hist)


def _rope_angles(pos):
    inv = ROPE_THETA ** (-jnp.arange(ROT_HALF, dtype=F32) / ROT_HALF)
    ang = pos.astype(F32)[:, None] * inv[None, :]
    return jnp.cos(ang), jnp.sin(ang)


def _rope_tables(pos):
    cos, sin = _rope_angles(pos)
    n = pos.shape[0]
    ones = jnp.ones((n, DA_DK - ROT_DIM), F32)
    zeros_h = jnp.zeros((n, ROT_HALF), F32)
    zeros_r = jnp.zeros((n, DA_DK - ROT_DIM), F32)
    cos64 = jnp.concatenate([cos, cos, ones], axis=1)
    sa64 = jnp.concatenate([-sin, zeros_h, zeros_r], axis=1)
    sb64 = jnp.concatenate([zeros_h, sin, zeros_r], axis=1)
    two = lambda a: jnp.concatenate([a, a], axis=1)
    return two(cos64), two(sa64), two(sb64)


def _post_attention(x, o, u, gv, w, mk, mv, mem_spec, hist, seq, tm, tag):
    t = x.shape[0]
    row2 = lambda a: a.reshape(1, -1)
    chunk = min(seq, GM_CHUNK)
    ws = w['gm_w_s'][:, :chunk, :chunk]
    bs = jnp.broadcast_to(w['gm_b'][:, :chunk, None], (GM_GROUPS, chunk, GM_CH))
    x = _mix_xattn(x, o.reshape(t, DA_WIDTH), u, gv, ws, bs, w['w_out'], row2(w['norm_x_g']), w['wq_c'],
                   row2(w['xq_norm_g']), mk, mv, mem_spec, w['wo_c'], min(tm, seq), chunk,
                   f"mix_xattn_{tag}")
    sub = min(SUB_ROWS, tm)
    return _conv_ffn(x, row2(w['norm_ffn_g']), w['w_up'], w['conv_w'], row2(w['conv_b']),
                     w['w_down'], hist, tm, sub, max(1, sub // seq), max(1, seq // tm), f"conv_ffn_{tag}")


def kernel(x_prompt, x_sample, cache_da_k, cache_da_v, cache_mem_k, cache_mem_v, state_ffn_conv, mem_prompt, norm_mix_g, w_in, da_q_norm_g, da_k_norm_g, lambda_q1, lambda_k1, lambda_q2, lambda_k2, da_subln_g, gm_norm_g, gm_w_s, gm_b, w_out, norm_x_g, norm_mem_g, wq_c, wk_c, wv_c, wo_c, xq_norm_g, xk_norm_g, norm_ffn_g, w_up, conv_w, conv_b, w_down):
    bp, sp, _ = x_prompt.shape
    bs_, ss, _ = x_sample.shape
    depth = w_in.shape[0]
    past = cache_da_k.shape[2]
    tm_w = min(WIDE_TILE, sp)
    tm_s = bs_ * ss
    tm_mem = min(TOKEN_TILE, bp * MEM_LEN)
    assert sp % tm_w == 0 and tm_w % SUB_ROWS == 0 and sp % Q_TILE == 0 and SUB_ROWS % GM_CHUNK == 0
    assert ss <= GM_CHUNK and tm_s % SUBLANES == 0 and ss >= CONV_W - 1 and ss % SUBLANES == 0
    assert (bp * MEM_LEN) % tm_mem == 0 and tm_mem % MEM_LEN == 0

    pos_p = jnp.arange(sp)
    rope_p = _rope_tables(pos_p)
    cos_p, sin_p = _rope_angles(pos_p)
    rope_kp = (cos_p.T, sin_p.T)
    rope_s = tuple(jnp.tile(a, (bs_, 1)) for a in _rope_tables(past + jnp.arange(ss)))
    cache_kt = jnp.transpose(cache_da_k, (0, 1, 3, 4, 5, 2)).reshape(depth, bs_, DA_QK_COLS, past)
    cache_v2 = cache_da_v.reshape(depth * bs_ * past * DA_HEADS, DA_DV)
    hist_p = jnp.zeros((bp, CONV_W - 1, D_FF), F32)
    mem_flat = mem_prompt.reshape(bp * MEM_LEN, D_MODEL)

    kt_all = lax.empty((depth, bp, DA_QK_COLS, sp), F32)
    v_all = lax.empty((depth * bp * sp * DA_HEADS, DA_DV), F32)
    cmk, cmv = _mem_rows_view(cache_mem_k), _mem_rows_view(cache_mem_v)

    xp = x_prompt.reshape(bp * sp, D_MODEL)
    xs = x_sample.reshape(tm_s, D_MODEL)
    outs = {n: [] for n in ('fc_p', 'dk_s', 'dv_s', 'gv_s', 'fc_s')}
    row2 = lambda a: a.reshape(1, -1)
    wb = {n: a.astype(BF16) for n, a in dict(w_in=w_in, w_out=w_out, wq_c=wq_c, wk_c=wk_c, wv_c=wv_c,
                                             wo_c=wo_c, w_up=w_up, w_down=w_down).items()}
    mk_all, mv_all, mkb, mvb = _mem_kv(mem_flat, norm_mem_g[:, None, :], wb['wk_c'], wb['wv_c'],
                                       xk_norm_g[:, None, :], tm_mem, "mem_kv")
    for l in range(depth):
        lam_init = 0.8 - 0.6 * math.exp(-0.3 * l)
        w = dict(gm_w_s=gm_w_s[l], gm_b=gm_b[l], w_out=(wb['w_out'], l),
                 norm_x_g=norm_x_g[l], wq_c=(wb['wq_c'], l), wo_c=(wb['wo_c'], l),
                 xq_norm_g=xq_norm_g[l], norm_ffn_g=norm_ffn_g[l], w_up=(wb['w_up'], l),
                 conv_w=conv_w[l], conv_b=conv_b[l], w_down=(wb['w_down'], l))
        w_in_l = (wb['w_in'], l)
        qg = jnp.tile(da_q_norm_g[l], 2).reshape(1, -1)
        kg = jnp.tile(da_k_norm_g[l], 2).reshape(1, -1)
        kg_col = jnp.broadcast_to(da_k_norm_g[l][:, None], (DA_DK, SUB_ROWS))
        gmg = jnp.tile(gm_norm_g[l], GM_GROUPS).reshape(1, -1)
        lams = [row2(a[l]) for a in (lambda_q1, lambda_k1, lambda_q2, lambda_k2)]
        subg = row2(da_subln_g[l])
        g_mix = row2(norm_mix_g[l])

        q, kt_all, v_all, vb, u, gv = _mix_in_prompt(xp, g_mix, w_in_l, qg, kg_col, gmg, rope_p, rope_kp,
                                                     kt_all, v_all, l, sp, tm_w, f"mix_in_p{l}")
        o = _da_prompt(q.reshape(bp, sp, DA_QK_COLS), kt_all, vb.reshape(bp, sp, DA_WIDTH), l, lams, subg,
                       lam_init, f"da_p{l}")
        mem_spec_p = pl.BlockSpec((None, MEM_LEN, D_MODEL), lambda i, l=l: (l, i // (sp // tm_w), 0))
        xp, cst = _post_attention(xp, o, u, gv, w, mkb, mvb, mem_spec_p, hist_p, sp, tm_w, f"p{l}")
        outs['fc_p'].append(cst)

        q, k, v, u, gv = _mix_in_sample(xs, g_mix, w_in_l, qg, kg, gmg, rope_s, f"mix_in_s{l}")
        o = _da_sample(q.reshape(bs_, ss, DA_QK_COLS), k.reshape(bs_, ss, DA_QK_COLS),
                       v.reshape(bs_, ss, DA_WIDTH), cache_kt, cache_v2, l, lams, subg, lam_init,
                       f"da_s{l}")
        mem_spec_s = pl.BlockSpec((MEM_ROWS, LANES), lambda i, l=l: (l * bs_ + i, 0))
        xs, cst = _post_attention(xs, o, u, gv, w, cmk, cmv, mem_spec_s, state_ffn_conv[l],
                                  ss, tm_s, f"s{l}")
        outs['dk_s'].append(k.reshape(bs_, ss, DA_HEADS, 2, DA_DK))
        outs['dv_s'].append(v.reshape(bs_, ss, DA_HEADS, DA_DV))
        outs['gv_s'].append(gv.reshape(bs_, ss, GM_GROUPS, GM_CH))
        outs['fc_s'].append(cst)

    st = {n: jnp.stack(a) for n, a in outs.items()}
    dk_p = jnp.transpose(kt_all.reshape(depth, bp, DA_HEADS, 2, DA_DK, sp), (0, 1, 5, 2, 3, 4))
    dv_p = v_all.reshape(depth, bp, sp, DA_HEADS, DA_DV)
    mk_p, mv_p = (_mem_rows_unview(a, (depth, bp)) for a in (mk_all, mv_all))
    return (xp.reshape(bp, sp, D_MODEL), xs.reshape(bs_, ss, D_MODEL), dk_p, dv_p, mk_p, mv_p,
            st['fc_p'], st['dk_s'], st['dv_s'], st['gv_s'], st['fc_s'])
```
